```python
import jax, jax.numpy as jnp
from jax import lax
import numpy as np

D_MODEL = 1024
BATCH = 16
SEQ = 256
DEPTH = 2
DEC_BATCH = 8
DEC_SEQ = 4096
PAST_LEN = 256

GRID_W = 64
N_MIXERS = 2
N_SSD_LAYERS = (DEPTH + 1) // 2
N_GDN_LAYERS = DEPTH // 2
CONV_K = 3
RMS_EPS = 1e-6

SSD_INNER = 2 * D_MODEL
SSD_HEADDIM = 64
SSD_HEADS = SSD_INNER // SSD_HEADDIM
SSD_GROUPS = 4
SSD_STATE = 128
SSD_CHUNK = 128
SSD_CONV_DIM = SSD_INNER + 2 * SSD_GROUPS * SSD_STATE
SSD_PROJ = SSD_INNER + SSD_CONV_DIM + 2 * SSD_HEADS

GDN_K_HEADS = 8
GDN_V_HEADS = 16
GDN_DK = 128
GDN_DV = 128
GDN_CHUNK = 64
GDN_KEY_DIM = GDN_K_HEADS * GDN_DK
GDN_VAL_DIM = GDN_V_HEADS * GDN_DV
GDN_CONV_DIM = 2 * GDN_KEY_DIM + GDN_VAL_DIM
GDN_PROJ = GDN_CONV_DIM + GDN_VAL_DIM + 4 * GDN_V_HEADS

N_EXPERTS = 16
N_EXPERT_GROUPS = 4
EXPERTS_PER_GROUP = N_EXPERTS // N_EXPERT_GROUPS
TOPK_GROUPS = 1
TOP_K = 2
D_EXPERT = 512

kernel_name = "hybrid_ssd_gdn_moe_diffusion_step"

F32 = jnp.float32


def rms_unit(x):
    xf = x.astype(F32)
    return xf * lax.rsqrt(jnp.mean(xf * xf, axis=-1, keepdims=True) + RMS_EPS)


def rmsnorm(x, g):
    return (rms_unit(x) * g.astype(F32)).astype(x.dtype)


def l2norm(x):
    return x * lax.rsqrt(jnp.sum(x * x, axis=-1, keepdims=True) + RMS_EPS)


def flip(t):
    return jnp.flip(t, axis=1)


def dwconv_centred(x, w, b):
    k = w.shape[0]
    y = lax.conv_general_dilated(x, w[:, None, :].astype(x.dtype), window_strides=(1,),
                                 padding=[(k // 2, k // 2)], dimension_numbers=('NWC', 'WIO', 'NWC'),
                                 feature_group_count=x.shape[-1])
    return y + b.astype(x.dtype)


def grid_pos_embed(n_tokens, d):
    t = jnp.arange(n_tokens)
    row = (t // GRID_W).astype(F32)
    col = (t % GRID_W).astype(F32)
    quarter = d // 4
    omega = 1.0 / (10000.0 ** (jnp.arange(quarter, dtype=F32) / quarter))
    def emb(pos):
        ang = pos[:, None] * omega[None, :]
        return jnp.concatenate([jnp.sin(ang), jnp.cos(ang)], axis=-1)
    return jnp.concatenate([emb(row), emb(col)], axis=-1)


def to_chunks(t, size):
    b, l = t.shape[:2]
    return jnp.moveaxis(t.reshape((b, l // size, size) + t.shape[2:]), 1, 0)


def from_chunks(t):
    nc, b, q = t.shape[:3]
    return jnp.moveaxis(t, 0, 1).reshape((b, nc * q) + t.shape[3:])


def ssd_scan(x, dt, a, bm, cm, s0):
    q = SSD_CHUNK
    rep = SSD_HEADS // SSD_GROUPS
    idx = jnp.arange(q)
    causal = (idx[:, None] >= idx[None, :])[None, :, :, None]
    def step(s, inp):
        xc, dtc, bc, cc = inp
        bc = jnp.repeat(bc, rep, axis=2)
        cc = jnp.repeat(cc, rep, axis=2)
        acs = jnp.cumsum(dtc * a, axis=1)
        seg = acs[:, :, None, :] - acs[:, None, :, :]
        decay = jnp.exp(jnp.where(causal, seg, -jnp.inf))
        scores = jnp.einsum('bihn,bjhn->bijh', cc, bc) * decay * dtc[:, None, :, :]
        y = jnp.einsum('bijh,bjhp->bihp', scores, xc)
        y = y + jnp.einsum('bihn,bhpn->bihp', cc, s) * jnp.exp(acs)[..., None]
        last = acs[:, -1]
        wj = dtc * jnp.exp(last[:, None, :] - acs)
        s_new = s * jnp.exp(last)[..., None, None] + jnp.einsum('bjhn,bjh,bjhp->bhpn', bc, wj, xc)
        return s_new, y
    s_fin, ys = lax.scan(step, s0, (to_chunks(x, q), to_chunks(dt, q), to_chunks(bm, q), to_chunks(cm, q)))
    return from_chunks(ys), s_fin


def ssd_mixer(h, s0, w_in, conv_w, conv_b, a_log, dt_bias, d_skip, norm_w, w_out):
    b, l, _ = h.shape
    proj = h @ w_in
    z, xbc, dt = jnp.split(proj, [SSD_INNER, SSD_INNER + SSD_CONV_DIM], axis=-1)
    xbc = jax.nn.silu(dwconv_centred(xbc, conv_w, conv_b)).astype(F32)
    xs, bm, cm = jnp.split(xbc, [SSD_INNER, SSD_INNER + SSD_GROUPS * SSD_STATE], axis=-1)
    xs = xs.reshape(b, l, SSD_HEADS, SSD_HEADDIM)
    bm = bm.reshape(b, l, SSD_GROUPS, SSD_STATE)
    cm = cm.reshape(b, l, SSD_GROUPS, SSD_STATE)
    dt = jax.nn.softplus(dt.astype(F32).reshape(b, l, 2, SSD_HEADS) + dt_bias.astype(F32))
    a = -jnp.exp(a_log.astype(F32))
    s0 = s0.astype(F32)
    y_f, s_f = ssd_scan(xs, dt[:, :, 0], a[0], bm, cm, s0[:, 0])
    y_b, s_b = ssd_scan(flip(xs), flip(dt[:, :, 1]), a[1], flip(bm), flip(cm), s0[:, 1])
    y = y_f + flip(y_b) + d_skip.astype(F32)[:, None] * xs
    y = y.reshape(b, l, SSD_INNER) * jax.nn.silu(z.astype(F32))
    y = rms_unit(y.reshape(b, l, SSD_GROUPS, SSD_INNER // SSD_GROUPS)).reshape(b, l, SSD_INNER)
    y = (y * norm_w.astype(F32)).astype(h.dtype)
    return y @ w_out, jnp.stack([s_f, s_b], axis=1)


def gdn_scan(q, k, v, g, beta, s0):
    c = GDN_CHUNK
    rep = GDN_V_HEADS // GDN_K_HEADS
    idx = jnp.arange(c)
    incl = idx[:, None] >= idx[None, :]
    strict = idx[:, None] > idx[None, :]
    eye = jnp.eye(c, dtype=F32)
    def step(s, inp):
        qc, kc, vc, gc, bc = inp
        qc = jnp.repeat(qc, rep, axis=2)
        kc = jnp.repeat(kc, rep, axis=2)
        gcum = jnp.cumsum(gc, axis=1)
        gh = jnp.moveaxis(gcum, 2, 1)
        seg = gh[..., :, None] - gh[..., None, :]
        dec = jnp.exp(jnp.where(incl, seg, -jnp.inf))
        kk = jnp.einsum('bihd,bjhd->bhij', kc, kc)
        m = jnp.where(strict, kk * dec, 0.0) * jnp.moveaxis(bc, 2, 1)[..., :, None]
        rhs = jnp.concatenate([vc * bc[..., None], kc * (bc * jnp.exp(gcum))[..., None]], axis=-1)
        sol = lax.linalg.triangular_solve(eye + m, jnp.moveaxis(rhs, 2, 1), left_side=True,
                                          lower=True, unit_diagonal=True)
        u = sol[..., :GDN_DV] - jnp.einsum('bhcd,bhde->bhce', sol[..., GDN_DV:], s)
        qk = jnp.einsum('bihd,bjhd->bhij', qc, kc) * dec
        o = jnp.einsum('bihd,bhde->bhie', qc * jnp.exp(gcum)[..., None], s) + jnp.einsum('bhij,bhje->bhie', qk, u)
        last = gcum[:, -1]
        s_new = s * jnp.exp(last)[..., None, None] + jnp.einsum('bjhd,bjh,bhje->bhde', kc, jnp.exp(last[:, None, :] - gcum), u)
        return s_new, jnp.moveaxis(o, 1, 2)
    s_fin, os_ = lax.scan(step, s0, (to_chunks(q, c), to_chunks(k, c), to_chunks(v, c), to_chunks(g, c), to_chunks(beta, c)))
    return from_chunks(os_), s_fin


def gdn_mixer(h, s0, w_in, conv_w, conv_b, a_log, dt_bias, norm_w, w_out):
    b, l, _ = h.shape
    proj = h @ w_in
    qkv, z, ba = jnp.split(proj, [GDN_CONV_DIM, GDN_CONV_DIM + GDN_VAL_DIM], axis=-1)
    qkv = jax.nn.silu(dwconv_centred(qkv, conv_w, conv_b)).astype(F32)
    q, k, v = jnp.split(qkv, [GDN_KEY_DIM, 2 * GDN_KEY_DIM], axis=-1)
    q = l2norm(q.reshape(b, l, GDN_K_HEADS, GDN_DK)) * (GDN_DK ** -0.5)
    k = l2norm(k.reshape(b, l, GDN_K_HEADS, GDN_DK))
    v = v.reshape(b, l, GDN_V_HEADS, GDN_DV)
    ba = ba.astype(F32).reshape(b, l, 2, 2, GDN_V_HEADS)
    beta = jax.nn.sigmoid(ba[:, :, :, 0])
    g = -jnp.exp(a_log.astype(F32)) * jax.nn.softplus(ba[:, :, :, 1] + dt_bias.astype(F32))
    s0 = s0.astype(F32)
    o_f, s_f = gdn_scan(q, k, v, g[:, :, 0], beta[:, :, 0], s0[:, 0])
    o_b, s_b = gdn_scan(flip(q), flip(k), flip(v), flip(g[:, :, 1]), flip(beta[:, :, 1]), s0[:, 1])
    o = o_f + flip(o_b)
    o = rms_unit(o) * norm_w.astype(F32) * jax.nn.silu(z.astype(F32).reshape(b, l, GDN_V_HEADS, GDN_DV))
    return o.reshape(b, l, GDN_VAL_DIM).astype(h.dtype) @ w_out, jnp.stack([s_f, s_b], axis=1)


def moe(h, w_router, router_bias, w_gate, w_up, w_down):
    shp = h.shape
    x = h.reshape(-1, shp[-1])
    aff = jax.nn.sigmoid(x.astype(F32) @ w_router.astype(F32))
    sel = aff + router_bias.astype(F32)
    grp_score = lax.top_k(sel.reshape(-1, N_EXPERT_GROUPS, EXPERTS_PER_GROUP), 2)[0].sum(-1)
    _, top_g = lax.top_k(grp_score, TOPK_GROUPS)
    gmask = jax.nn.one_hot(top_g, N_EXPERT_GROUPS, dtype=F32).sum(-2) > 0
    emask = jnp.repeat(gmask, EXPERTS_PER_GROUP, axis=-1)
    _, top_e = lax.top_k(jnp.where(emask, sel, -jnp.inf), TOP_K)
    wts = jnp.take_along_axis(aff, top_e, axis=-1)
    wts = wts / jnp.sum(wts, axis=-1, keepdims=True)
    gates = jnp.einsum('tke,tk->te', jax.nn.one_hot(top_e, N_EXPERTS, dtype=F32), wts)
    def body(acc, p):
        wg, wu, wd, ge = p
        hid = jax.nn.silu(x @ wg) * (x @ wu)
        return acc + (hid @ wd).astype(F32) * ge[:, None], None
    acc, _ = lax.scan(body, jnp.zeros(x.shape, F32), (w_gate, w_up, w_down, gates.T))
    return acc.astype(h.dtype).reshape(shp)


def trunk(x, cond, st_ssd, st_gdn, w_ada, b_ada, norm_w, final_norm_w,
          ssd_w_in, ssd_conv_w, ssd_conv_b, ssd_a_log, ssd_dt_bias, ssd_d, ssd_norm_w, ssd_w_out,
          gdn_w_in, gdn_conv_w, gdn_conv_b, gdn_a_log, gdn_dt_bias, gdn_norm_w, gdn_w_out,
          w_router, router_bias, moe_w_gate, moe_w_up, moe_w_down):
    ssd_states, gdn_states = [], []
    for i in range(DEPTH):
        ada = (jax.nn.silu(cond.astype(F32)) @ w_ada[i].astype(F32) + b_ada[i].astype(F32))[:, None, :]
        sh1, sc1, g1, sh2, sc2, g2 = jnp.split(ada, 6, axis=-1)
        hm = (rmsnorm(x, norm_w[i, 0]) * (1.0 + sc1) + sh1).astype(x.dtype)
        j = i // N_MIXERS
        if i % N_MIXERS == 0:
            out, s = ssd_mixer(hm, st_ssd[:, j], ssd_w_in[j], ssd_conv_w[j], ssd_conv_b[j], ssd_a_log[j],
                               ssd_dt_bias[j], ssd_d[j], ssd_norm_w[j], ssd_w_out[j])
            ssd_states.append(s)
        else:
            out, s = gdn_mixer(hm, st_gdn[:, j], gdn_w_in[j], gdn_conv_w[j], gdn_conv_b[j], gdn_a_log[j],
                               gdn_dt_bias[j], gdn_norm_w[j], gdn_w_out[j])
            gdn_states.append(s)
        x = x + (g1 * out.astype(F32)).astype(x.dtype)
        hf = (rmsnorm(x, norm_w[i, 1]) * (1.0 + sc2) + sh2).astype(x.dtype)
        x = x + (g2 * moe(hf, w_router, router_bias, moe_w_gate[i], moe_w_up[i], moe_w_down[i]).astype(F32)).astype(x.dtype)
    return rmsnorm(x, final_norm_w), jnp.stack(ssd_states, axis=1), jnp.stack(gdn_states, axis=1)


def setup_inputs(seed: int = 0) -> dict:
    key = jax.random.key(seed)
    ks = iter(jax.random.split(key, 40))
    def nrm(shape, scale):
        return jax.random.normal(next(ks), shape, F32) * scale
    def gain(shape):
        return 1.0 + nrm(shape, 0.02)
    def a_log_init(shape):
        return jnp.log(jax.random.uniform(next(ks), shape, F32, 1.0, 16.0))
    def dt_bias_init(shape):
        dt = jnp.exp(jax.random.uniform(next(ks), shape, F32, np.log(1e-3), np.log(1e-1)))
        return dt + jnp.log(-jnp.expm1(-dt))
    D = D_MODEL
    return {
        "x_prompt": nrm((BATCH, SEQ, D), 1.0),
        "x_sample": nrm((DEC_BATCH, DEC_SEQ, D), 1.0),
        "state_ssd": nrm((DEC_BATCH, N_SSD_LAYERS, 2, SSD_HEADS, SSD_HEADDIM, SSD_STATE), 0.5),
        "state_gdn": nrm((DEC_BATCH, N_GDN_LAYERS, 2, GDN_V_HEADS, GDN_DK, GDN_DV), 0.1),
        "c": nrm((DEC_BATCH, D), 1.0),
        "c_ctx": nrm((D,), 1.0),
        "w_ada": nrm((DEPTH, D, 6 * D), 0.5 * D ** -0.5),
        "b_ada": nrm((DEPTH, 6 * D), 0.01),
        "norm_w": gain((DEPTH, 2, D)),
        "final_norm_w": gain((D,)),
        "ssd_w_in": nrm((N_SSD_LAYERS, D, SSD_PROJ), D ** -0.5),
        "ssd_conv_w": nrm((N_SSD_LAYERS, CONV_K, SSD_CONV_DIM), CONV_K ** -0.5),
        "ssd_conv_b": nrm((N_SSD_LAYERS, SSD_CONV_DIM), 0.01),
        "ssd_a_log": a_log_init((N_SSD_LAYERS, 2, SSD_HEADS)),
        "ssd_dt_bias": dt_bias_init((N_SSD_LAYERS, 2, SSD_HEADS)),
        "ssd_d": gain((N_SSD_LAYERS, SSD_HEADS)),
        "ssd_norm_w": gain((N_SSD_LAYERS, SSD_INNER)),
        "ssd_w_out": nrm((N_SSD_LAYERS, SSD_INNER, D), SSD_INNER ** -0.5),
        "gdn_w_in": nrm((N_GDN_LAYERS, D, GDN_PROJ), D ** -0.5),
        "gdn_conv_w": nrm((N_GDN_LAYERS, CONV_K, GDN_CONV_DIM), CONV_K ** -0.5),
        "gdn_conv_b": nrm((N_GDN_LAYERS, GDN_CONV_DIM), 0.01),
        "gdn_a_log": a_log_init((N_GDN_LAYERS, 2, GDN_V_HEADS)),
        "gdn_dt_bias": dt_bias_init((N_GDN_LAYERS, 2, GDN_V_HEADS)),
        "gdn_norm_w": gain((N_GDN_LAYERS, GDN_DV)),
        "gdn_w_out": nrm((N_GDN_LAYERS, GDN_VAL_DIM, D), GDN_VAL_DIM ** -0.5),
        "w_router": nrm((D, N_EXPERTS), D ** -0.5),
        "router_bias": nrm((N_EXPERTS,), 0.01),
        "moe_w_gate": nrm((DEPTH, N_EXPERTS, D, D_EXPERT), D ** -0.5),
        "moe_w_up": nrm((DEPTH, N_EXPERTS, D, D_EXPERT), D ** -0.5),
        "moe_w_down": nrm((DEPTH, N_EXPERTS, D_EXPERT, D), D_EXPERT ** -0.5),
    }


def reference(x_prompt, x_sample, state_ssd, state_gdn, c, c_ctx, w_ada, b_ada, norm_w, final_norm_w,
              ssd_w_in, ssd_conv_w, ssd_conv_b, ssd_a_log, ssd_dt_bias, ssd_d, ssd_norm_w, ssd_w_out,
              gdn_w_in, gdn_conv_w, gdn_conv_b, gdn_a_log, gdn_dt_bias, gdn_norm_w, gdn_w_out,
              w_router, router_bias, moe_w_gate, moe_w_up, moe_w_down):
    weights = (w_ada, b_ada, norm_w, final_norm_w,
               ssd_w_in, ssd_conv_w, ssd_conv_b, ssd_a_log, ssd_dt_bias, ssd_d, ssd_norm_w, ssd_w_out,
               gdn_w_in, gdn_conv_w, gdn_conv_b, gdn_a_log, gdn_dt_bias, gdn_norm_w, gdn_w_out,
               w_router, router_bias, moe_w_gate, moe_w_up, moe_w_down)
    nb = x_prompt.shape[0]
    zero_ssd = jnp.zeros((nb, N_SSD_LAYERS, 2, SSD_HEADS, SSD_HEADDIM, SSD_STATE), F32)
    zero_gdn = jnp.zeros((nb, N_GDN_LAYERS, 2, GDN_V_HEADS, GDN_DK, GDN_DV), F32)
    y_prompt, new_state_ssd, new_state_gdn = trunk(x_prompt, c_ctx[None, :], zero_ssd, zero_gdn, *weights)
    n_lat = x_sample.shape[1]
    rows = n_lat // GRID_W
    pos = grid_pos_embed(rows * GRID_W, D_MODEL)
    xs = x_sample + pos.astype(x_sample.dtype)[None]
    y_sample, _, _ = trunk(xs, c, state_ssd, state_gdn, *weights)
    return (y_prompt, y_sample, new_state_ssd, new_state_gdn)
```

```python
import functools

import jax
import jax.numpy as jnp
from jax import lax
from jax.experimental import pallas as pl
from jax.experimental.pallas import tpu as pltpu

F32 = jnp.float32
BF16 = jnp.bfloat16
HIGHEST = lax.Precision.HIGHEST

D_MODEL = 1024
GRID_W = 64
RMS_EPS = 1e-6

SSD_INNER = 2 * D_MODEL
SSD_HEADDIM = 64
SSD_HEADS = SSD_INNER // SSD_HEADDIM
SSD_GROUPS = 4
SSD_STATE = 128
SSD_HEADS_PER_GROUP = SSD_HEADS // SSD_GROUPS
SSD_PAIRS_PER_GROUP = SSD_HEADS_PER_GROUP // 2
SSD_GROUP_WIDTH = SSD_INNER // SSD_GROUPS
SSD_CONV_DIM = SSD_INNER + 2 * SSD_GROUPS * SSD_STATE
SSD_MAIN = SSD_INNER + SSD_CONV_DIM

GDN_K_HEADS = 8
GDN_V_HEADS = 16
GDN_DK = 128
GDN_DV = 128
GDN_KEY_DIM = GDN_K_HEADS * GDN_DK
GDN_VAL_DIM = GDN_V_HEADS * GDN_DV
GDN_CONV_DIM = 2 * GDN_KEY_DIM + GDN_VAL_DIM
GDN_MAIN = GDN_CONV_DIM + GDN_VAL_DIM

N_EXPERTS = 16
EXPERTS_PER_GROUP = 4
N_EXPERT_GROUPS = N_EXPERTS // EXPERTS_PER_GROUP
D_EXPERT = 512

LANES = 128
CHUNK = 128
BF16_ROWS = 16
N_MOD_ROWS = 16
VMEM_LIMIT = 48 * 1024 * 1024


def _cparams(sem):
    return pltpu.CompilerParams(dimension_semantics=sem, vmem_limit_bytes=VMEM_LIMIT)


def _tile(target, *dims):
    t = target
    while any(d % t for d in dims):
        t //= 2
    assert t >= 8
    return t


def _silu(x):
    return x * jax.nn.sigmoid(x)


def _softplus(x):
    return jnp.maximum(x, 0.0) + jnp.log1p(jnp.exp(-jnp.abs(x)))


def _rms_unit(x):
    return x * lax.rsqrt(jnp.mean(x * x, axis=-1, keepdims=True) + RMS_EPS)


def _dot(a, b, **kw):
    return jnp.dot(a, b, preferred_element_type=F32, **kw)


def _dot_nt(a, b, **kw):
    return lax.dot_general(a, b, (((1,), (1,)), ((), ())), preferred_element_type=F32, **kw)


def _dot_tn(a, b):
    return lax.dot_general(a, b, (((0,), (0,)), ((), ())), preferred_element_type=F32)


def _embed_kernel(xp_ref, xs_ref, pos_ref, o_ref, *, n_ctx_tiles):
    i = pl.program_id(0)

    @pl.when(i < n_ctx_tiles)
    def _():
        o_ref[...] = xp_ref[...]

    @pl.when(i >= n_ctx_tiles)
    def _():
        o_ref[...] = xs_ref[...] + pos_ref[...]


def _grid_pos_embed(n_tokens, d):
    t = jnp.arange(n_tokens)
    row = (t // GRID_W).astype(F32)
    col = (t % GRID_W).astype(F32)
    quarter = d // 4
    omega = 1.0 / (10000.0 ** (jnp.arange(quarter, dtype=F32) / quarter))

    def emb(pos):
        ang = pos[:, None] * omega[None, :]
        return jnp.concatenate([jnp.sin(ang), jnp.cos(ang)], axis=-1)

    return jnp.concatenate([emb(row), emb(col)], axis=-1)


def _embed(x_prompt, x_sample):
    nb, lc, d = x_prompt.shape
    ns, ls, _ = x_sample.shape
    tc, ts = nb * lc, ns * ls
    tm = _tile(1024, tc, ls)
    nct, tiles_per_seq = tc // tm, ls // tm
    pos = _grid_pos_embed((ls // GRID_W) * GRID_W, d)
    return pl.pallas_call(
        functools.partial(_embed_kernel, n_ctx_tiles=nct),
        grid=((tc + ts) // tm,),
        in_specs=[
            pl.BlockSpec((tm, d), lambda i: (jnp.minimum(i, nct - 1), 0)),
            pl.BlockSpec((tm, d), lambda i: (jnp.maximum(i - nct, 0), 0)),
            pl.BlockSpec((tm, d), lambda i: (jnp.maximum(i - nct, 0) % tiles_per_seq, 0)),
        ],
        out_specs=pl.BlockSpec((tm, d), lambda i: (i, 0)),
        out_shape=jax.ShapeDtypeStruct((tc + ts, d), F32),
        compiler_params=_cparams(("arbitrary",)),
        name="embed",
    )(x_prompt.reshape(tc, d), x_sample.reshape(ts, d), pos)


def _ada_kernel(c_ref, w_ref, b_ref, o_ref):
    s = _silu(c_ref[...])
    o_ref[...] = _dot(s, w_ref[...], precision=HIGHEST) + b_ref[...]


def _ada(cond, w_ada, b_ada):
    depth, d, six_d = w_ada.shape
    return pl.pallas_call(
        _ada_kernel,
        grid=(depth, six_d // d),
        in_specs=[
            pl.BlockSpec((N_MOD_ROWS, d), lambda l, j: (0, 0)),
            pl.BlockSpec((None, d, d), lambda l, j: (l, 0, j)),
            pl.BlockSpec((None, 1, d), lambda l, j: (l, 0, j)),
        ],
        out_specs=pl.BlockSpec((None, N_MOD_ROWS, d), lambda l, j: (l, 0, j)),
        out_shape=jax.ShapeDtypeStruct((depth, N_MOD_ROWS, six_d), F32),
        compiler_params=_cparams(("arbitrary", "arbitrary")),
        name="ada",
    )(cond, w_ada, b_ada.reshape(depth, 1, six_d))


def _proj_kernel(x_ref, mod_ref, nw_ref, w_ref, wst_ref, o_ref, ost_ref, hm_ref):
    @pl.when(pl.program_id(1) == 0)
    def _():
        hm = _rms_unit(x_ref[...]) * nw_ref[...] * (1.0 + mod_ref[1:2, :]) + mod_ref[0:1, :]
        hmb = hm.astype(BF16)
        hm_ref[...] = hmb
        ost_ref[...] = _dot_nt(wst_ref[...], hmb)

    o_ref[...] = _dot(hm_ref[...], w_ref[...]).astype(BF16)


def _proj(x, mods, layer, seg_of, norm_w, w_main, w_small_t):
    t, d = x.shape
    n_main = w_main.shape[1]
    n_small = w_small_t.shape[0]
    tm = seg_of.tile
    tn = _tile(1024, n_main)
    return pl.pallas_call(
        _proj_kernel,
        grid=(t // tm, n_main // tn),
        in_specs=[
            pl.BlockSpec((tm, d), lambda i, j: (i, 0)),
            pl.BlockSpec((None, None, 6, d), lambda i, j: (layer, seg_of(i), 0, 0)),
            pl.BlockSpec((1, d), lambda i, j: (0, 0)),
            pl.BlockSpec((d, tn), lambda i, j: (0, j)),
            pl.BlockSpec((n_small, d), lambda i, j: (0, 0)),
        ],
        out_specs=[
            pl.BlockSpec((tm, tn), lambda i, j: (i, j)),
            pl.BlockSpec((n_small, tm), lambda i, j: (0, i)),
        ],
        out_shape=[
            jax.ShapeDtypeStruct((t, n_main), BF16),
            jax.ShapeDtypeStruct((n_small, t), F32),
        ],
        scratch_shapes=[pltpu.VMEM((tm, d), BF16)],
        compiler_params=_cparams(("arbitrary", "arbitrary")),
        name="proj",
    )(x, mods, norm_w.reshape(1, d), w_main, w_small_t)


class _SegOf:
    def __init__(self, tile, n_ctx_tokens, sample_len):
        assert n_ctx_tokens % tile == 0 and sample_len % tile == 0
        self.tile = tile
        self.n_ctx_tiles = n_ctx_tokens // tile
        self.tiles_per_sample = sample_len // tile

    def __call__(self, i):
        return jnp.where(i < self.n_ctx_tiles, 0, 1 + (i - self.n_ctx_tiles) // self.tiles_per_sample)


def _chunk_consts():
    ii = lax.broadcasted_iota(jnp.int32, (CHUNK, CHUNK), 0)
    jj = lax.broadcasted_iota(jnp.int32, (CHUNK, CHUNK), 1)
    return ii, jj


def _conv_silu(src_ref, c, n_chunks, w_ref, b_ref):
    seq_len = n_chunks * CHUNK
    r0 = pl.multiple_of(c * CHUNK, CHUNK)
    xb = src_ref[pl.ds(r0, CHUNK), :].astype(F32)
    lo = pl.multiple_of(jnp.maximum(r0 - BF16_ROWS, 0), BF16_ROWS)
    hi = pl.multiple_of(jnp.minimum(r0 + CHUNK, seq_len - BF16_ROWS), BF16_ROWS)
    prev_row = src_ref[pl.ds(lo, BF16_ROWS), :].astype(F32)[BF16_ROWS - 1:BF16_ROWS, :]
    next_row = src_ref[pl.ds(hi, BF16_ROWS), :].astype(F32)[0:1, :]
    prev_row = prev_row * jnp.where(c > 0, 1.0, 0.0)
    next_row = next_row * jnp.where(c < n_chunks - 1, 1.0, 0.0)
    row = lax.broadcasted_iota(jnp.int32, xb.shape, 0)
    xp = jnp.where(row == 0, prev_row, pltpu.roll(xb, 1, 0))
    xn = jnp.where(row == CHUNK - 1, next_row, pltpu.roll(xb, CHUNK - 1, 0))
    y = w_ref[0:1, :] * xp + w_ref[1:2, :] * xb + w_ref[2:3, :] * xn + b_ref[...]
    return _silu(y)


def _rows_to_cols(*row_blocks):
    n = sum(b.shape[0] for b in row_blocks)
    pad = jnp.zeros((LANES - n, CHUNK), F32)
    return jnp.concatenate(list(row_blocks) + [pad], axis=0).T


def _ssd_kernel(*refs, n_chunks, has_s0, emit_state):
    it = iter(refs)
    x_ref, b_ref, c_ref, rows_ref, bias_ref, alog_ref = (next(it) for _ in range(6))
    cwx_ref, cwb_ref, cwc_ref, cbx_ref, cbb_ref, cbc_ref, dsk_ref = (next(it) for _ in range(7))
    s0_ref = next(it) if has_s0 else None
    y_ref = next(it)
    sout_ref = next(it) if emit_state else None
    yf_ref, st_ref = next(it), next(it)

    ii, jj = _chunk_consts()
    lane = lax.broadcasted_iota(jnp.int32, (CHUNK, LANES), 1)
    first_head = lane < SSD_HEADDIM
    nh = SSD_HEADS_PER_GROUP

    if has_s0:
        st_ref[...] = s0_ref[...]
    else:
        st_ref[...] = jnp.zeros(st_ref.shape, F32)

    def step(c, d):
        keep = (ii >= jj) if d == 0 else (ii <= jj)
        cum = ((ii <= jj) if d == 0 else (ii >= jj)).astype(F32)
        edge = CHUNK - 1 if d == 0 else 0
        r0 = pl.multiple_of(c * CHUNK, CHUNK)
        xc = _conv_silu(x_ref, c, n_chunks, cwx_ref, cbx_ref)
        bb = _conv_silu(b_ref, c, n_chunks, cwb_ref, cbb_ref).astype(BF16)
        cb = _conv_silu(c_ref, c, n_chunks, cwc_ref, cbc_ref).astype(BF16)
        g = _dot_nt(cb, bb)
        sl = slice(nh * d, nh * d + nh)
        dt = _softplus(rows_ref[sl, pl.ds(r0, CHUNK)] + bias_ref[sl, :])
        dta = dt * (-jnp.exp(alog_ref[sl, :]))
        acs = _dot(dta, cum, precision=HIGHEST)
        logdt = jnp.log(dt)
        cols = _rows_to_cols(acs, dt)
        ys = []
        for p in range(SSD_PAIRS_PER_GROUP):
            xpair = xc[:, LANES * p:LANES * (p + 1)]
            xpb = xpair.astype(BF16)
            y_intra = None
            for k in range(2):
                h = 2 * p + k
                seg = cols[:, h:h + 1] - acs[h:h + 1, :] + logdt[h:h + 1, :]
                decay = jnp.exp(jnp.where(keep, seg, -jnp.inf))
                mh = (g * decay).astype(BF16)
                xh = jnp.where(first_head if k == 0 else ~first_head, xpb, jnp.zeros_like(xpb))
                t = _dot(mh, xh)
                y_intra = t if y_intra is None else y_intra + t
            h0, h1 = 2 * p, 2 * p + 1
            acs_c = jnp.where(first_head, cols[:, h0:h0 + 1], cols[:, h1:h1 + 1])
            dt_c = jnp.where(first_head, cols[:, nh + h0:nh + h0 + 1], cols[:, nh + h1:nh + h1 + 1])
            last = jnp.where(first_head[0:1, :], acs[h0:h0 + 1, edge:edge + 1], acs[h1:h1 + 1, edge:edge + 1])
            st = st_ref[d, p]
            y = y_intra + _dot(cb, st.astype(BF16)) * jnp.exp(acs_c)
            xt = (xpair * (jnp.exp(last - acs_c) * dt_c)).astype(BF16)
            st_ref[d, p] = st * jnp.exp(last) + _dot_tn(bb, xt)
            ys.append(y)
        return r0, xc, jnp.concatenate(ys, axis=1)

    def fwd_body(c, carry):
        r0, _, y = step(c, 0)
        yf_ref[pl.ds(r0, CHUNK), :] = y
        return carry

    def bwd_body(k, carry):
        c = n_chunks - 1 - k
        r0, xc, y = step(c, 1)
        y_ref[pl.ds(r0, CHUNK), :] = (yf_ref[pl.ds(r0, CHUNK), :] + y + dsk_ref[...] * xc).astype(BF16)
        return carry

    lax.fori_loop(0, n_chunks, fwd_body, 0)
    lax.fori_loop(0, n_chunks, bwd_body, 0)
    if emit_state:
        sout_ref[...] = st_ref[...]


def _ssd_scan(p_main, rows, prm, s0, n_seq, seq_len, row_block0, emit_state):
    t = p_main.shape[0]
    gw = SSD_GROUP_WIDTH
    x0 = SSD_INNER // gw
    b0 = (2 * SSD_INNER) // SSD_STATE
    c0 = b0 + SSD_GROUPS
    cx0 = 0
    cb0 = SSD_INNER // SSD_STATE
    cc0 = cb0 + SSD_GROUPS
    has_s0 = s0 is not None
    rb = lambda b: row_block0 + b
    st_block = (None, 2, None, SSD_PAIRS_PER_GROUP, SSD_STATE, LANES)
    st_map = lambda b, g: (b, 0, g, 0, 0, 0)
    in_specs = [
        pl.BlockSpec((seq_len, gw), lambda b, g: (rb(b), x0 + g)),
        pl.BlockSpec((seq_len, SSD_STATE), lambda b, g: (rb(b), b0 + g)),
        pl.BlockSpec((seq_len, SSD_STATE), lambda b, g: (rb(b), c0 + g)),
        pl.BlockSpec((None, 2 * SSD_HEADS_PER_GROUP, seq_len), lambda b, g: (g, 0, rb(b))),
        pl.BlockSpec((None, 2 * SSD_HEADS_PER_GROUP, LANES), lambda b, g: (g, 0, 0)),
        pl.BlockSpec((None, 2 * SSD_HEADS_PER_GROUP, LANES), lambda b, g: (g, 0, 0)),
        pl.BlockSpec((3, gw), lambda b, g: (0, cx0 + g)),
        pl.BlockSpec((3, SSD_STATE), lambda b, g: (0, cb0 + g)),
        pl.BlockSpec((3, SSD_STATE), lambda b, g: (0, cc0 + g)),
        pl.BlockSpec((1, gw), lambda b, g: (0, cx0 + g)),
        pl.BlockSpec((1, SSD_STATE), lambda b, g: (0, cb0 + g)),
        pl.BlockSpec((1, SSD_STATE), lambda b, g: (0, cc0 + g)),
        pl.BlockSpec((1, gw), lambda b, g: (0, g)),
    ]
    args = [p_main, p_main, p_main, rows, prm["bias"], prm["alog"],
            prm["conv_w"], prm["conv_w"], prm["conv_w"], prm["conv_b"], prm["conv_b"], prm["conv_b"], prm["dskip"]]
    if has_s0:
        in_specs.append(pl.BlockSpec(st_block, st_map))
        args.append(s0)
    out_specs = [pl.BlockSpec((seq_len, gw), lambda b, g: (b, g))]
    out_shape = [jax.ShapeDtypeStruct((n_seq * seq_len, SSD_INNER), BF16)]
    if emit_state:
        out_specs.append(pl.BlockSpec(st_block, st_map))
        out_shape.append(jax.ShapeDtypeStruct(
            (n_seq, 2, SSD_GROUPS, SSD_PAIRS_PER_GROUP, SSD_STATE, LANES), F32))
    assert t % seq_len == 0 and seq_len % CHUNK == 0
    outs = pl.pallas_call(
        functools.partial(_ssd_kernel, n_chunks=seq_len // CHUNK, has_s0=has_s0, emit_state=emit_state),
        grid=(n_seq, SSD_GROUPS),
        in_specs=in_specs,
        out_specs=out_specs,
        out_shape=out_shape,
        scratch_shapes=[
            pltpu.VMEM((seq_len, gw), F32),
            pltpu.VMEM((2, SSD_PAIRS_PER_GROUP, SSD_STATE, LANES), F32),
        ],
        compiler_params=_cparams(("arbitrary", "arbitrary")),
        name="ssd_scan",
    )(*args)
    return outs if emit_state else (outs[0], None)


def _merge_masks(ii, jj):
    masks = []
    s = 1
    while s < CHUNK:
        masks.append(((ii // s) != (jj // s)) & ((ii // (2 * s)) == (jj // (2 * s))))
        s *= 2
    return masks


def _unit_tri_inverse_minus_eye(m, masks):
    n = -jnp.where(masks[0], m, 0.0)
    for mask in masks[1:]:
        lo = jnp.where(mask, m, 0.0)
        lb, nb = lo.astype(BF16), n.astype(BF16)
        y = lo + _dot(nb, lb)
        n = n - (y + _dot(y.astype(BF16), nb))
    return n


def _gdn_kernel(*refs, n_chunks, has_s0, emit_state):
    it = iter(refs)
    q_ref, k_ref, v_ref, rows_ref, bias_ref, alog_ref = (next(it) for _ in range(6))
    cwq_ref, cwk_ref, cwv_ref, cbq_ref, cbk_ref, cbv_ref = (next(it) for _ in range(6))
    s0_ref = next(it) if has_s0 else None
    o_ref = next(it)
    sout_ref = next(it) if emit_state else None
    of_ref, st_ref = next(it), next(it)

    ii, jj = _chunk_consts()
    masks = _merge_masks(ii, jj)

    if has_s0:
        st_ref[...] = s0_ref[...]
    else:
        st_ref[...] = jnp.zeros(st_ref.shape, F32)

    def step(c, d):
        keep = (ii >= jj) if d == 0 else (ii <= jj)
        strict = (ii > jj) if d == 0 else (ii < jj)
        cum = ((ii <= jj) if d == 0 else (ii >= jj)).astype(F32)
        edge = CHUNK - 1 if d == 0 else 0
        r0 = pl.multiple_of(c * CHUNK, CHUNK)
        qc = _conv_silu(q_ref, c, n_chunks, cwq_ref, cbq_ref)
        kc = _conv_silu(k_ref, c, n_chunks, cwk_ref, cbk_ref)
        vc = _conv_silu(v_ref, c, n_chunks, cwv_ref, cbv_ref)
        qn = qc * (lax.rsqrt(jnp.sum(qc * qc, axis=-1, keepdims=True) + RMS_EPS) * (GDN_DK ** -0.5))
        kn = kc * lax.rsqrt(jnp.sum(kc * kc, axis=-1, keepdims=True) + RMS_EPS)
        kb = kn.astype(BF16)
        kk = _dot_nt(kb, kb)
        qk = _dot_nt(qn.astype(BF16), kb)
        raw = rows_ref[:, pl.ds(r0, CHUNK)]
        beta = jax.nn.sigmoid(raw)
        gl = -jnp.exp(alog_ref[...]) * _softplus(raw + bias_ref[...])
        gcum = _dot(gl, cum, precision=HIGHEST)
        cols = _rows_to_cols(beta, gcum)
        os_ = []
        for j in range(2):
            rb_ = 2 * d + j
            rg = 4 + 2 * d + j
            b_c = cols[:, rb_:rb_ + 1]
            g_c = cols[:, 8 + rg:8 + rg + 1]
            g_r = gcum[rg:rg + 1, :]
            dec = jnp.exp(jnp.where(keep, g_c - g_r, -jnp.inf))
            m = jnp.where(strict, kk * dec, 0.0) * b_c
            n = _unit_tri_inverse_minus_eye(m, masks)
            eg = jnp.exp(g_c)
            rhs = jnp.concatenate([vc[:, GDN_DV * j:GDN_DV * (j + 1)] * b_c, kn * (b_c * eg)], axis=1)
            sol = rhs + _dot(n.astype(BF16), rhs.astype(BF16))
            st = st_ref[d, j]
            sb = st.astype(BF16)
            u = sol[:, :GDN_DV] - _dot(sol[:, GDN_DV:].astype(BF16), sb)
            ub = u.astype(BF16)
            o = _dot((qn * eg).astype(BF16), sb) + _dot((qk * dec).astype(BF16), ub)
            last = gcum[rg:rg + 1, edge:edge + 1]
            kt = (kn * jnp.exp(last - g_c)).astype(BF16)
            st_ref[d, j] = st * jnp.exp(last) + _dot_tn(kt, ub)
            os_.append(o)
        return r0, jnp.concatenate(os_, axis=1)

    def fwd_body(c, carry):
        r0, o = step(c, 0)
        of_ref[pl.ds(r0, CHUNK), :] = o
        return carry

    def bwd_body(k, carry):
        r0, o = step(n_chunks - 1 - k, 1)
        o_ref[pl.ds(r0, CHUNK), :] = (of_ref[pl.ds(r0, CHUNK), :] + o).astype(BF16)
        return carry

    lax.fori_loop(0, n_chunks, fwd_body, 0)
    lax.fori_loop(0, n_chunks, bwd_body, 0)
    if emit_state:
        sout_ref[...] = st_ref[...]


def _gdn_scan(p_main, rows, prm, s0, n_seq, seq_len, row_block0, emit_state):
    vw = 2 * GDN_DV
    k0 = GDN_KEY_DIM // GDN_DK
    v0 = (2 * GDN_KEY_DIM) // vw
    has_s0 = s0 is not None
    rb = lambda b: row_block0 + b
    st_block = (None, None, 2, 2, GDN_DK, GDN_DV)
    st_map = lambda b, h: (b, 0, 0, h, 0, 0)
    in_specs = [
        pl.BlockSpec((seq_len, GDN_DK), lambda b, h: (rb(b), h)),
        pl.BlockSpec((seq_len, GDN_DK), lambda b, h: (rb(b), k0 + h)),
        pl.BlockSpec((seq_len, vw), lambda b, h: (rb(b), v0 + h)),
        pl.BlockSpec((None, 8, seq_len), lambda b, h: (h, 0, rb(b))),
        pl.BlockSpec((None, 8, LANES), lambda b, h: (h, 0, 0)),
        pl.BlockSpec((None, 8, LANES), lambda b, h: (h, 0, 0)),
        pl.BlockSpec((3, GDN_DK), lambda b, h: (0, h)),
        pl.BlockSpec((3, GDN_DK), lambda b, h: (0, k0 + h)),
        pl.BlockSpec((3, vw), lambda b, h: (0, v0 + h)),
        pl.BlockSpec((1, GDN_DK), lambda b, h: (0, h)),
        pl.BlockSpec((1, GDN_DK), lambda b, h: (0, k0 + h)),
        pl.BlockSpec((1, vw), lambda b, h: (0, v0 + h)),
    ]
    args = [p_main, p_main, p_main, rows, prm["bias"], prm["alog"],
            prm["conv_w"], prm["conv_w"], prm["conv_w"], prm["conv_b"], prm["conv_b"], prm["conv_b"]]
    if has_s0:
        in_specs.append(pl.BlockSpec(st_block, st_map))
        args.append(s0)
    out_specs = [pl.BlockSpec((seq_len, vw), lambda b, h: (b, h))]
    out_shape = [jax.ShapeDtypeStruct((n_seq * seq_len, GDN_VAL_DIM), BF16)]
    if emit_state:
        out_specs.append(pl.BlockSpec(st_block, st_map))
        out_shape.append(jax.ShapeDtypeStruct((n_seq, 1, 2, GDN_V_HEADS, GDN_DK, GDN_DV), F32))
    assert seq_len % CHUNK == 0
    outs = pl.pallas_call(
        functools.partial(_gdn_kernel, n_chunks=seq_len // CHUNK, has_s0=has_s0, emit_state=emit_state),
        grid=(n_seq, GDN_K_HEADS),
        in_specs=in_specs,
        out_specs=out_specs,
        out_shape=out_shape,
        scratch_shapes=[
            pltpu.VMEM((seq_len, vw), F32),
            pltpu.VMEM((2, 2, GDN_DK, GDN_DV), F32),
        ],
        compiler_params=_cparams(("arbitrary", "arbitrary")),
        name="gdn_scan",
    )(*args)
    return outs if emit_state else (outs[0], None)


def _route(logits_t, rbias):
    aff = jax.nn.sigmoid(logits_t)
    sel = aff + rbias
    a = [aff[e:e + 1, :] for e in range(N_EXPERTS)]
    s = [sel[e:e + 1, :] for e in range(N_EXPERTS)]
    gs = []
    for g in range(N_EXPERT_GROUPS):
        v = s[EXPERTS_PER_GROUP * g:EXPERTS_PER_GROUP * (g + 1)]
        best = None
        for x in range(EXPERTS_PER_GROUP):
            for y in range(x + 1, EXPERTS_PER_GROUP):
                ps = v[x] + v[y]
                best = ps if best is None else jnp.maximum(best, ps)
        gs.append(best)
    top, gi = gs[0], jnp.zeros_like(gs[0], dtype=jnp.int32)
    for g in range(1, N_EXPERT_GROUPS):
        up = gs[g] > top
        top = jnp.where(up, gs[g], top)
        gi = jnp.where(up, g, gi)
    ms = [jnp.where(gi == e // EXPERTS_PER_GROUP, s[e], -jnp.inf) for e in range(N_EXPERTS)]
    b1, i1 = ms[0], jnp.zeros_like(gi)
    for e in range(1, N_EXPERTS):
        up = ms[e] > b1
        b1 = jnp.where(up, ms[e], b1)
        i1 = jnp.where(up, e, i1)
    b2, i2 = jnp.full_like(b1, -jnp.inf), jnp.zeros_like(gi)
    for e in range(N_EXPERTS):
        cand = jnp.where(i1 == e, -jnp.inf, ms[e])
        up = cand > b2
        b2 = jnp.where(up, cand, b2)
        i2 = jnp.where(up, e, i2)
    w1 = sum(jnp.where(i1 == e, a[e], 0.0) for e in range(N_EXPERTS))
    w2 = sum(jnp.where(i2 == e, a[e], 0.0) for e in range(N_EXPERTS))
    tot = w1 + w2
    rows = [jnp.where(i1 == e, w1, 0.0) / tot + jnp.where(i2 == e, w2, 0.0) / tot for e in range(N_EXPERTS)]
    return jnp.concatenate(rows, axis=0)


def _post_kernel(y_ref, z_ref, nw_ref, wo_ref, x_ref, mod_ref, n2_ref, wrt_ref, rb_ref,
                 x1_ref, hf_ref, gate_ref, *, group_width, gate_before_norm):
    y = y_ref[...].astype(F32)
    gz = _silu(z_ref[...].astype(F32))
    if gate_before_norm:
        y = y * gz
    parts = [_rms_unit(y[:, s:s + group_width]) for s in range(0, y.shape[1], group_width)]
    yn = jnp.concatenate(parts, axis=1) * nw_ref[...]
    if not gate_before_norm:
        yn = yn * gz
    out = _dot(yn.astype(BF16), wo_ref[...])
    x1 = x_ref[...] + mod_ref[2:3, :] * out
    x1_ref[...] = x1
    hf = _rms_unit(x1) * n2_ref[...] * (1.0 + mod_ref[4:5, :]) + mod_ref[3:4, :]
    hf_ref[...] = hf.astype(BF16)
    logits_t = _dot_nt(wrt_ref[...], hf, precision=HIGHEST)
    gates_t = _route(logits_t, rb_ref[...])
    pad = jnp.zeros((LANES - N_EXPERTS, gates_t.shape[1]), F32)
    gate_ref[...] = jnp.concatenate([gates_t, pad], axis=0).T


def _post(y, p_main, z_block0, x, mods, layer, seg_of, norm_w_full, w_out, norm2_w, w_router_t, router_bias,
          group_width, gate_before_norm):
    t, d = x.shape
    inner = y.shape[1]
    tm = seg_of.tile
    return pl.pallas_call(
        functools.partial(_post_kernel, group_width=group_width, gate_before_norm=gate_before_norm),
        grid=(t // tm,),
        in_specs=[
            pl.BlockSpec((tm, inner), lambda i: (i, 0)),
            pl.BlockSpec((tm, inner), lambda i: (i, z_block0)),
            pl.BlockSpec((1, inner), lambda i: (0, 0)),
            pl.BlockSpec((inner, d), lambda i: (0, 0)),
            pl.BlockSpec((tm, d), lambda i: (i, 0)),
            pl.BlockSpec((None, None, 6, d), lambda i: (layer, seg_of(i), 0, 0)),
            pl.BlockSpec((1, d), lambda i: (0, 0)),
            pl.BlockSpec((N_EXPERTS, d), lambda i: (0, 0)),
            pl.BlockSpec((N_EXPERTS, 1), lambda i: (0, 0)),
        ],
        out_specs=[
            pl.BlockSpec((tm, d), lambda i: (i, 0)),
            pl.BlockSpec((tm, d), lambda i: (i, 0)),
            pl.BlockSpec((tm, LANES), lambda i: (i, 0)),
        ],
        out_shape=[
            jax.ShapeDtypeStruct((t, d), F32),
            jax.ShapeDtypeStruct((t, d), BF16),
            jax.ShapeDtypeStruct((t, LANES), F32),
        ],
        compiler_params=_cparams(("arbitrary",)),
        name="post",
    )(y, p_main, norm_w_full.reshape(1, inner), w_out, x, mods, norm2_w.reshape(1, d), w_router_t,
      router_bias.reshape(N_EXPERTS, 1))


def _moe_kernel(hf_ref, gate_ref, wg_ref, wu_ref, wd_ref, x1_ref, mod_ref, fn_ref, o_ref, acc_ref, *, final_norm):
    e = pl.program_id(1)

    @pl.when(e == 0)
    def _():
        acc_ref[...] = jnp.zeros(acc_ref.shape, F32)

    h = hf_ref[...]
    hid = _silu(_dot(h, wg_ref[...])) * _dot(h, wu_ref[...])
    y = _dot(hid.astype(BF16), wd_ref[...])
    lane = lax.broadcasted_iota(jnp.int32, gate_ref.shape, 1)
    gcol = jnp.sum(jnp.where(lane == e, gate_ref[...], 0.0), axis=1, keepdims=True)
    acc_ref[...] += y * gcol

    @pl.when(e == pl.num_programs(1) - 1)
    def _():
        x2 = x1_ref[...] + mod_ref[5:6, :] * acc_ref[...]
        if final_norm:
            x2 = _rms_unit(x2) * fn_ref[...]
        o_ref[...] = x2


def _moe(hf, gates, wg, wu, wd, x1, mods, layer, seg_of, final_norm_w, final_norm):
    t, d = x1.shape
    n_e, _, de = wg.shape
    tm = seg_of.tile
    return pl.pallas_call(
        functools.partial(_moe_kernel, final_norm=final_norm),
        grid=(t // tm, n_e),
        in_specs=[
            pl.BlockSpec((tm, d), lambda i, e: (i, 0)),
            pl.BlockSpec((tm, LANES), lambda i, e: (i, 0)),
            pl.BlockSpec((None, d, de), lambda i, e: (e, 0, 0)),
            pl.BlockSpec((None, d, de), lambda i, e: (e, 0, 0)),
            pl.BlockSpec((None, de, d), lambda i, e: (e, 0, 0)),
            pl.BlockSpec((tm, d), lambda i, e: (i, 0)),
            pl.BlockSpec((None, None, 6, d), lambda i, e: (layer, seg_of(i), 0, 0)),
            pl.BlockSpec((1, d), lambda i, e: (0, 0)),
        ],
        out_specs=pl.BlockSpec((tm, d), lambda i, e: (i, 0)),
        out_shape=jax.ShapeDtypeStruct((t, d), F32),
        scratch_shapes=[pltpu.VMEM((tm, d), F32)],
        compiler_params=_cparams(("arbitrary", "arbitrary")),
        name="moe",
    )(hf, gates, wg, wu, wd, x1, mods, final_norm_w.reshape(1, d))


def _lanes(v):
    return jnp.broadcast_to(v[..., None].astype(F32), v.shape + (LANES,))


def _ssd_small_order():
    idx = []
    for g in range(SSD_GROUPS):
        for d in range(2):
            for h in range(SSD_HEADS_PER_GROUP):
                idx.append(d * SSD_HEADS + SSD_HEADS_PER_GROUP * g + h)
    return jnp.array(idx, jnp.int32)


def _gdn_small_order():
    idx = []
    for kh in range(GDN_K_HEADS):
        for which in range(2):
            for d in range(2):
                for j in range(2):
                    idx.append(d * 2 * GDN_V_HEADS + which * GDN_V_HEADS + 2 * kh + j)
    return jnp.array(idx, jnp.int32)


def _ssd_params(j, ssd_w_in, ssd_conv_w, ssd_conv_b, ssd_a_log, ssd_dt_bias, ssd_d):
    order = _ssd_small_order()
    w = ssd_w_in[j]
    per_row = lambda v: v.reshape(-1)[order].reshape(SSD_GROUPS, 2 * SSD_HEADS_PER_GROUP)
    return dict(
        w_main=w[:, :SSD_MAIN].astype(BF16),
        w_small_t=w[:, SSD_MAIN:][:, order].T.astype(BF16),
        conv_w=ssd_conv_w[j], conv_b=ssd_conv_b[j].reshape(1, -1),
        bias=_lanes(per_row(ssd_dt_bias[j])), alog=_lanes(per_row(ssd_a_log[j])),
        dskip=jnp.repeat(ssd_d[j].astype(F32), SSD_HEADDIM).reshape(1, SSD_INNER),
    )


def _gdn_params(j, gdn_w_in, gdn_conv_w, gdn_conv_b, gdn_a_log, gdn_dt_bias):
    order = _gdn_small_order()
    w = gdn_w_in[j]
    def per_row(v):
        full = jnp.stack([jnp.zeros_like(v), v], axis=1).astype(F32)
        return full.reshape(-1)[order].reshape(GDN_K_HEADS, 8)
    return dict(
        w_main=w[:, :GDN_MAIN].astype(BF16),
        w_small_t=w[:, GDN_MAIN:][:, order].T.astype(BF16),
        conv_w=gdn_conv_w[j], conv_b=gdn_conv_b[j].reshape(1, -1),
        bias=_lanes(per_row(gdn_dt_bias[j])), alog=_lanes(per_row(gdn_a_log[j])),
    )


def _ssd_state_in(s):
    n = s.shape[0]
    s = s.reshape(n, 2, SSD_GROUPS, SSD_PAIRS_PER_GROUP, 2, SSD_HEADDIM, SSD_STATE)
    return s.transpose(0, 1, 2, 3, 6, 4, 5).reshape(n, 2, SSD_GROUPS, SSD_PAIRS_PER_GROUP, SSD_STATE, LANES)


def _ssd_state_out(s):
    n = s.shape[0]
    s = s.reshape(n, 2, SSD_GROUPS, SSD_PAIRS_PER_GROUP, SSD_STATE, 2, SSD_HEADDIM)
    return s.transpose(0, 1, 2, 3, 5, 6, 4).reshape(n, 2, SSD_HEADS, SSD_HEADDIM, SSD_STATE)


def kernel(x_prompt, x_sample, state_ssd, state_gdn, c, c_ctx, w_ada, b_ada, norm_w, final_norm_w, ssd_w_in, ssd_conv_w, ssd_conv_b, ssd_a_log, ssd_dt_bias, ssd_d, ssd_norm_w, ssd_w_out, gdn_w_in, gdn_conv_w, gdn_conv_b, gdn_a_log, gdn_dt_bias, gdn_norm_w, gdn_w_out, w_router, router_bias, moe_w_gate, moe_w_up, moe_w_down):
    nb, lc, d = x_prompt.shape
    ns, ls, _ = x_sample.shape
    depth = w_ada.shape[0]
    tc, ts = nb * lc, ns * ls
    assert tc % ls == 0 and ns + 1 <= N_MOD_ROWS

    x = _embed(x_prompt, x_sample)
    cond = jnp.concatenate([c_ctx[None, :], c, jnp.zeros((N_MOD_ROWS - 1 - ns, d), F32)], axis=0)
    mods = _ada(cond, w_ada, b_ada).reshape(depth, N_MOD_ROWS, 6, d)

    seg_big = _SegOf(_tile(1024, tc, ls), tc, ls)
    seg_mid = _SegOf(_tile(512, tc, ls), tc, ls)
    w_router_t = w_router.T.astype(F32)
    ssd_states, gdn_states = [], []
    for i in range(depth):
        j = i // 2
        if i % 2 == 0:
            prm = _ssd_params(j, ssd_w_in, ssd_conv_w, ssd_conv_b, ssd_a_log, ssd_dt_bias, ssd_d)
            p_main, p_small_t = _proj(x, mods, i, seg_big, norm_w[i, 0], prm["w_main"], prm["w_small_t"])
            rows = p_small_t.reshape(SSD_GROUPS, 2 * SSD_HEADS_PER_GROUP, tc + ts)
            y_ctx, s_ctx = _ssd_scan(p_main, rows, prm, None, nb, lc, 0, True)
            y_smp, _ = _ssd_scan(p_main, rows, prm, _ssd_state_in(state_ssd[:, j].astype(F32)), ns, ls,
                                 tc // ls, False)
            ssd_states.append(_ssd_state_out(s_ctx))
            y = jnp.concatenate([y_ctx, y_smp], axis=0)
            x, hf, gates = _post(y, p_main, 0, x, mods, i, seg_mid, ssd_norm_w[j], ssd_w_out[j].astype(BF16),
                                 norm_w[i, 1], w_router_t, router_bias, SSD_GROUP_WIDTH, True)
        else:
            prm = _gdn_params(j, gdn_w_in, gdn_conv_w, gdn_conv_b, gdn_a_log, gdn_dt_bias)
            p_main, p_small_t = _proj(x, mods, i, seg_big, norm_w[i, 0], prm["w_main"], prm["w_small_t"])
            rows = p_small_t.reshape(GDN_K_HEADS, 8, tc + ts)
            o_ctx, s_ctx = _gdn_scan(p_main, rows, prm, None, nb, lc, 0, True)
            o_smp, _ = _gdn_scan(p_main, rows, prm, state_gdn[:, j:j + 1].astype(F32), ns, ls, tc // ls, False)
            gdn_states.append(s_ctx[:, 0])
            y = jnp.concatenate([o_ctx, o_smp], axis=0)
            x, hf, gates = _post(y, p_main, GDN_CONV_DIM // GDN_VAL_DIM, x, mods, i, seg_mid,
                                 jnp.tile(gdn_norm_w[j], GDN_V_HEADS), gdn_w_out[j].astype(BF16),
                                 norm_w[i, 1], w_router_t, router_bias, GDN_DV, False)
        x = _moe(hf, gates, moe_w_gate[i].astype(BF16), moe_w_up[i].astype(BF16), moe_w_down[i].astype(BF16),
                 x, mods, i, seg_big, final_norm_w, i == depth - 1)
    y_prompt = x[:tc].reshape(nb, lc, d)
    y_sample = x[tc:].reshape(ns, ls, d)
    return (y_prompt, y_sample, jnp.stack(ssd_states, axis=1), jnp.stack(gdn_states, axis=1))
```

```python
import functools

import jax
import jax.numpy as jnp
from jax import lax
from jax.experimental import pallas as pl
from jax.experimental.pallas import tpu as pltpu

F32 = jnp.float32
BF16 = jnp.bfloat16
HIGHEST = lax.Precision.HIGHEST

D_MODEL = 1024
GRID_W = 64
RMS_EPS = 1e-6

SSD_INNER = 2 * D_MODEL
SSD_HEADDIM = 64
SSD_HEADS = SSD_INNER // SSD_HEADDIM
SSD_GROUPS = 4
SSD_STATE = 128
SSD_HEADS_PER_GROUP = SSD_HEADS // SSD_GROUPS
SSD_PAIRS_PER_GROUP = SSD_HEADS_PER_GROUP // 2
SSD_GROUP_WIDTH = SSD_INNER // SSD_GROUPS
SSD_CONV_DIM = SSD_INNER + 2 * SSD_GROUPS * SSD_STATE
SSD_MAIN = SSD_INNER + SSD_CONV_DIM

GDN_K_HEADS = 8
GDN_V_HEADS = 16
GDN_DK = 128
GDN_DV = 128
GDN_KEY_DIM = GDN_K_HEADS * GDN_DK
GDN_VAL_DIM = GDN_V_HEADS * GDN_DV
GDN_CONV_DIM = 2 * GDN_KEY_DIM + GDN_VAL_DIM
GDN_MAIN = GDN_CONV_DIM + GDN_VAL_DIM

N_EXPERTS = 16
EXPERTS_PER_GROUP = 4
N_EXPERT_GROUPS = N_EXPERTS // EXPERTS_PER_GROUP
D_EXPERT = 512

LANES = 128
CHUNK = 128
GDN_PREP_CHUNKS = 2
assert GDN_DK == CHUNK
BF16_ROWS = 16
N_MOD_ROWS = 16
VMEM_LIMIT = 48 * 1024 * 1024


def _cparams(sem):
    return pltpu.CompilerParams(dimension_semantics=sem, vmem_limit_bytes=VMEM_LIMIT)


def _tile(target, *dims):
    t = target
    while any(d % t for d in dims):
        t //= 2
    assert t >= 8
    return t


def _silu(x):
    return x * jax.nn.sigmoid(x)


def _softplus(x):
    return jnp.maximum(x, 0.0) + jnp.log1p(jnp.exp(-jnp.abs(x)))


def _rms_unit(x):
    return x * lax.rsqrt(jnp.mean(x * x, axis=-1, keepdims=True) + RMS_EPS)


def _dot(a, b, **kw):
    return jnp.dot(a, b, preferred_element_type=F32, **kw)


def _dot_nt(a, b, **kw):
    return lax.dot_general(a, b, (((1,), (1,)), ((), ())), preferred_element_type=F32, **kw)


def _dot_tn(a, b):
    return lax.dot_general(a, b, (((0,), (0,)), ((), ())), preferred_element_type=F32)


def _embed_kernel(xp_ref, xs_ref, pos_ref, o_ref, *, n_ctx_tiles):
    i = pl.program_id(0)

    @pl.when(i < n_ctx_tiles)
    def _():
        o_ref[...] = xp_ref[...]

    @pl.when(i >= n_ctx_tiles)
    def _():
        o_ref[...] = xs_ref[...] + pos_ref[...]


def _grid_pos_embed(n_tokens, d):
    t = jnp.arange(n_tokens)
    row = (t // GRID_W).astype(F32)
    col = (t % GRID_W).astype(F32)
    quarter = d // 4
    omega = 1.0 / (10000.0 ** (jnp.arange(quarter, dtype=F32) / quarter))

    def emb(pos):
        ang = pos[:, None] * omega[None, :]
        return jnp.concatenate([jnp.sin(ang), jnp.cos(ang)], axis=-1)

    return jnp.concatenate([emb(row), emb(col)], axis=-1)


def _embed(x_prompt, x_sample):
    nb, lc, d = x_prompt.shape
    ns, ls, _ = x_sample.shape
    tc, ts = nb * lc, ns * ls
    tm = _tile(1024, tc, ls)
    nct, tiles_per_seq = tc // tm, ls // tm
    pos = _grid_pos_embed((ls // GRID_W) * GRID_W, d)
    return pl.pallas_call(
        functools.partial(_embed_kernel, n_ctx_tiles=nct),
        grid=((tc + ts) // tm,),
        in_specs=[
            pl.BlockSpec((tm, d), lambda i: (jnp.minimum(i, nct - 1), 0)),
            pl.BlockSpec((tm, d), lambda i: (jnp.maximum(i - nct, 0), 0)),
            pl.BlockSpec((tm, d), lambda i: (jnp.maximum(i - nct, 0) % tiles_per_seq, 0)),
        ],
        out_specs=pl.BlockSpec((tm, d), lambda i: (i, 0)),
        out_shape=jax.ShapeDtypeStruct((tc + ts, d), F32),
        compiler_params=_cparams(("arbitrary",)),
        name="embed",
    )(x_prompt.reshape(tc, d), x_sample.reshape(ts, d), pos)


def _ada_kernel(c_ref, w_ref, b_ref, o_ref):
    s = _silu(c_ref[...])
    o_ref[...] = _dot(s, w_ref[...], precision=HIGHEST) + b_ref[...]


def _ada(cond, w_ada, b_ada):
    depth, d, six_d = w_ada.shape
    return pl.pallas_call(
        _ada_kernel,
        grid=(depth, six_d // d),
        in_specs=[
            pl.BlockSpec((N_MOD_ROWS, d), lambda l, j: (0, 0)),
            pl.BlockSpec((None, d, d), lambda l, j: (l, 0, j)),
            pl.BlockSpec((None, 1, d), lambda l, j: (l, 0, j)),
        ],
        out_specs=pl.BlockSpec((None, N_MOD_ROWS, d), lambda l, j: (l, 0, j)),
        out_shape=jax.ShapeDtypeStruct((depth, N_MOD_ROWS, six_d), F32),
        compiler_params=_cparams(("arbitrary", "arbitrary")),
        name="ada",
    )(cond, w_ada, b_ada.reshape(depth, 1, six_d))


def _proj_kernel(x_ref, mod_ref, nw_ref, w_ref, wst_ref, o_ref, ost_ref, hm_ref):
    @pl.when(pl.program_id(1) == 0)
    def _():
        hm = _rms_unit(x_ref[...]) * nw_ref[...] * (1.0 + mod_ref[1:2, :]) + mod_ref[0:1, :]
        hmb = hm.astype(BF16)
        hm_ref[...] = hmb
        ost_ref[...] = _dot_nt(wst_ref[...], hmb)

    o_ref[...] = _dot(hm_ref[...], w_ref[...]).astype(BF16)


def _proj(x, mods, layer, seg_of, norm_w, w_main, w_small_t):
    t, d = x.shape
    n_main = w_main.shape[1]
    n_small = w_small_t.shape[0]
    tm = seg_of.tile
    tn = _tile(1024, n_main)
    return pl.pallas_call(
        _proj_kernel,
        grid=(t // tm, n_main // tn),
        in_specs=[
            pl.BlockSpec((tm, d), lambda i, j: (i, 0)),
            pl.BlockSpec((None, None, 6, d), lambda i, j: (layer, seg_of(i), 0, 0)),
            pl.BlockSpec((1, d), lambda i, j: (0, 0)),
            pl.BlockSpec((d, tn), lambda i, j: (0, j)),
            pl.BlockSpec((n_small, d), lambda i, j: (0, 0)),
        ],
        out_specs=[
            pl.BlockSpec((tm, tn), lambda i, j: (i, j)),
            pl.BlockSpec((n_small, tm), lambda i, j: (0, i)),
        ],
        out_shape=[
            jax.ShapeDtypeStruct((t, n_main), BF16),
            jax.ShapeDtypeStruct((n_small, t), F32),
        ],
        scratch_shapes=[pltpu.VMEM((tm, d), BF16)],
        compiler_params=_cparams(("arbitrary", "arbitrary")),
        name="proj",
    )(x, mods, norm_w.reshape(1, d), w_main, w_small_t)


class _SegOf:
    def __init__(self, tile, n_ctx_tokens, sample_len):
        assert n_ctx_tokens % tile == 0 and sample_len % tile == 0
        self.tile = tile
        self.n_ctx_tiles = n_ctx_tokens // tile
        self.tiles_per_sample = sample_len // tile

    def __call__(self, i):
        return jnp.where(i < self.n_ctx_tiles, 0, 1 + (i - self.n_ctx_tiles) // self.tiles_per_sample)


def _chunk_consts():
    ii = lax.broadcasted_iota(jnp.int32, (CHUNK, CHUNK), 0)
    jj = lax.broadcasted_iota(jnp.int32, (CHUNK, CHUNK), 1)
    return ii, jj


def _conv_silu(src_ref, c, n_chunks, w_ref, b_ref):
    seq_len = n_chunks * CHUNK
    r0 = pl.multiple_of(c * CHUNK, CHUNK)
    xb = src_ref[pl.ds(r0, CHUNK), :].astype(F32)
    lo = pl.multiple_of(jnp.maximum(r0 - BF16_ROWS, 0), BF16_ROWS)
    hi = pl.multiple_of(jnp.minimum(r0 + CHUNK, seq_len - BF16_ROWS), BF16_ROWS)
    prev_row = src_ref[pl.ds(lo, BF16_ROWS), :].astype(F32)[BF16_ROWS - 1:BF16_ROWS, :]
    next_row = src_ref[pl.ds(hi, BF16_ROWS), :].astype(F32)[0:1, :]
    prev_row = prev_row * jnp.where(c > 0, 1.0, 0.0)
    next_row = next_row * jnp.where(c < n_chunks - 1, 1.0, 0.0)
    row = lax.broadcasted_iota(jnp.int32, xb.shape, 0)
    xp = jnp.where(row == 0, prev_row, pltpu.roll(xb, 1, 0))
    xn = jnp.where(row == CHUNK - 1, next_row, pltpu.roll(xb, CHUNK - 1, 0))
    y = w_ref[0:1, :] * xp + w_ref[1:2, :] * xb + w_ref[2:3, :] * xn + b_ref[...]
    return _silu(y)


def _rows_to_cols(*row_blocks):
    n = sum(b.shape[0] for b in row_blocks)
    pad = jnp.zeros((LANES - n, CHUNK), F32)
    return jnp.concatenate(list(row_blocks) + [pad], axis=0).T


def _ssd_kernel(*refs, n_chunks, has_s0, emit_state):
    it = iter(refs)
    x_ref, b_ref, c_ref, rows_ref, bias_ref, alog_ref = (next(it) for _ in range(6))
    cwx_ref, cwb_ref, cwc_ref, cbx_ref, cbb_ref, cbc_ref, dsk_ref = (next(it) for _ in range(7))
    s0_ref = next(it) if has_s0 else None
    y_ref = next(it)
    sout_ref = next(it) if emit_state else None
    yf_ref, st_ref = next(it), next(it)

    ii, jj = _chunk_consts()
    lane = lax.broadcasted_iota(jnp.int32, (CHUNK, LANES), 1)
    first_head = lane < SSD_HEADDIM
    nh = SSD_HEADS_PER_GROUP

    if has_s0:
        st_ref[...] = s0_ref[...]
    else:
        st_ref[...] = jnp.zeros(st_ref.shape, F32)

    def step(c, d):
        keep = (ii >= jj) if d == 0 else (ii <= jj)
        cum = ((ii <= jj) if d == 0 else (ii >= jj)).astype(F32)
        edge = CHUNK - 1 if d == 0 else 0
        r0 = pl.multiple_of(c * CHUNK, CHUNK)
        xc = _conv_silu(x_ref, c, n_chunks, cwx_ref, cbx_ref)
        bb = _conv_silu(b_ref, c, n_chunks, cwb_ref, cbb_ref).astype(BF16)
        cb = _conv_silu(c_ref, c, n_chunks, cwc_ref, cbc_ref).astype(BF16)
        g = _dot_nt(cb, bb)
        sl = slice(nh * d, nh * d + nh)
        dt = _softplus(rows_ref[sl, pl.ds(r0, CHUNK)] + bias_ref[sl, :])
        dta = dt * (-jnp.exp(alog_ref[sl, :]))
        acs = _dot(dta, cum, precision=HIGHEST)
        logdt = jnp.log(dt)
        cols = _rows_to_cols(acs, dt)
        ys = []
        for p in range(SSD_PAIRS_PER_GROUP):
            xpair = xc[:, LANES * p:LANES * (p + 1)]
            xpb = xpair.astype(BF16)
            y_intra = None
            for k in range(2):
                h = 2 * p + k
                seg = cols[:, h:h + 1] - acs[h:h + 1, :] + logdt[h:h + 1, :]
                decay = jnp.exp(jnp.where(keep, seg, -jnp.inf))
                mh = (g * decay).astype(BF16)
                xh = jnp.where(first_head if k == 0 else ~first_head, xpb, jnp.zeros_like(xpb))
                t = _dot(mh, xh)
                y_intra = t if y_intra is None else y_intra + t
            h0, h1 = 2 * p, 2 * p + 1
            acs_c = jnp.where(first_head, cols[:, h0:h0 + 1], cols[:, h1:h1 + 1])
            dt_c = jnp.where(first_head, cols[:, nh + h0:nh + h0 + 1], cols[:, nh + h1:nh + h1 + 1])
            last = jnp.where(first_head[0:1, :], acs[h0:h0 + 1, edge:edge + 1], acs[h1:h1 + 1, edge:edge + 1])
            st = st_ref[d, p]
            y = y_intra + _dot(cb, st.astype(BF16)) * jnp.exp(acs_c)
            xt = (xpair * (jnp.exp(last - acs_c) * dt_c)).astype(BF16)
            st_ref[d, p] = st * jnp.exp(last) + _dot_tn(bb, xt)
            ys.append(y)
        return r0, xc, jnp.concatenate(ys, axis=1)

    def fwd_body(c, carry):
        r0, _, y = step(c, 0)
        yf_ref[pl.ds(r0, CHUNK), :] = y
        return carry

    def bwd_body(k, carry):
        c = n_chunks - 1 - k
        r0, xc, y = step(c, 1)
        y_ref[pl.ds(r0, CHUNK), :] = (yf_ref[pl.ds(r0, CHUNK), :] + y + dsk_ref[...] * xc).astype(BF16)
        return carry

    lax.fori_loop(0, n_chunks, fwd_body, 0)
    lax.fori_loop(0, n_chunks, bwd_body, 0)
    if emit_state:
        sout_ref[...] = st_ref[...]


def _ssd_scan(p_main, rows, prm, s0, n_seq, seq_len, row_block0, emit_state):
    t = p_main.shape[0]
    gw = SSD_GROUP_WIDTH
    x0 = SSD_INNER // gw
    b0 = (2 * SSD_INNER) // SSD_STATE
    c0 = b0 + SSD_GROUPS
    cx0 = 0
    cb0 = SSD_INNER // SSD_STATE
    cc0 = cb0 + SSD_GROUPS
    has_s0 = s0 is not None
    rb = lambda b: row_block0 + b
    st_block = (None, 2, None, SSD_PAIRS_PER_GROUP, SSD_STATE, LANES)
    st_map = lambda b, g: (b, 0, g, 0, 0, 0)
    in_specs = [
        pl.BlockSpec((seq_len, gw), lambda b, g: (rb(b), x0 + g)),
        pl.BlockSpec((seq_len, SSD_STATE), lambda b, g: (rb(b), b0 + g)),
        pl.BlockSpec((seq_len, SSD_STATE), lambda b, g: (rb(b), c0 + g)),
        pl.BlockSpec((None, 2 * SSD_HEADS_PER_GROUP, seq_len), lambda b, g: (g, 0, rb(b))),
        pl.BlockSpec((None, 2 * SSD_HEADS_PER_GROUP, LANES), lambda b, g: (g, 0, 0)),
        pl.BlockSpec((None, 2 * SSD_HEADS_PER_GROUP, LANES), lambda b, g: (g, 0, 0)),
        pl.BlockSpec((3, gw), lambda b, g: (0, cx0 + g)),
        pl.BlockSpec((3, SSD_STATE), lambda b, g: (0, cb0 + g)),
        pl.BlockSpec((3, SSD_STATE), lambda b, g: (0, cc0 + g)),
        pl.BlockSpec((1, gw), lambda b, g: (0, cx0 + g)),
        pl.BlockSpec((1, SSD_STATE), lambda b, g: (0, cb0 + g)),
        pl.BlockSpec((1, SSD_STATE), lambda b, g: (0, cc0 + g)),
        pl.BlockSpec((1, gw), lambda b, g: (0, g)),
    ]
    args = [p_main, p_main, p_main, rows, prm["bias"], prm["alog"],
            prm["conv_w"], prm["conv_w"], prm["conv_w"], prm["conv_b"], prm["conv_b"], prm["conv_b"], prm["dskip"]]
    if has_s0:
        in_specs.append(pl.BlockSpec(st_block, st_map))
        args.append(s0)
    out_specs = [pl.BlockSpec((seq_len, gw), lambda b, g: (b, g))]
    out_shape = [jax.ShapeDtypeStruct((n_seq * seq_len, SSD_INNER), BF16)]
    if emit_state:
        out_specs.append(pl.BlockSpec(st_block, st_map))
        out_shape.append(jax.ShapeDtypeStruct(
            (n_seq, 2, SSD_GROUPS, SSD_PAIRS_PER_GROUP, SSD_STATE, LANES), F32))
    assert t % seq_len == 0 and seq_len % CHUNK == 0
    outs = pl.pallas_call(
        functools.partial(_ssd_kernel, n_chunks=seq_len // CHUNK, has_s0=has_s0, emit_state=emit_state),
        grid=(n_seq, SSD_GROUPS),
        in_specs=in_specs,
        out_specs=out_specs,
        out_shape=out_shape,
        scratch_shapes=[
            pltpu.VMEM((seq_len, gw), F32),
            pltpu.VMEM((2, SSD_PAIRS_PER_GROUP, SSD_STATE, LANES), F32),
        ],
        compiler_params=_cparams(("arbitrary", "arbitrary")),
        name="ssd_scan",
    )(*args)
    return outs if emit_state else (outs[0], None)


def _merge_masks(ii, jj):
    masks = []
    s = 1
    while s < CHUNK:
        masks.append(((ii // s) != (jj // s)) & ((ii // (2 * s)) == (jj // (2 * s))))
        s *= 2
    return masks


def _gdn_kernel(*refs, n_chunks, has_s0, emit_state):
    it = iter(refs)
    q_ref, k_ref, v_ref, rows_ref, bias_ref, alog_ref = (next(it) for _ in range(6))
    cwq_ref, cwk_ref, cwv_ref, cbq_ref, cbk_ref, cbv_ref = (next(it) for _ in range(6))
    s0_ref = next(it) if has_s0 else None
    o_ref = next(it)
    sout_ref = next(it) if emit_state else None
    u0_ref, w_ref, qq_ref, kt_ref, el_ref, part_ref, mk_ref, st_ref = (next(it) for _ in range(8))

    ii, jj = _chunk_consts()
    eye = (ii == jj).astype(F32)
    masks = _merge_masks(ii, jj)
    for lvl, mask in enumerate(masks[1:]):
        mk_ref[lvl] = mask.astype(BF16)
    keep = ((ii >= jj), (ii <= jj))
    cum = ((ii <= jj).astype(F32), (ii >= jj).astype(F32))
    edge = (CHUNK - 1, 0)
    chains = [(d, j) for d in range(2) for j in range(2)]

    if has_s0:
        st_ref[...] = s0_ref[...]
    else:
        st_ref[...] = jnp.zeros(st_ref.shape, F32)

    def prepare(p, carry):
        work = []
        for cc in range(GDN_PREP_CHUNKS):
            c = p * GDN_PREP_CHUNKS + cc
            rows = pl.ds(pl.multiple_of(c * CHUNK, CHUNK), CHUNK)
            qc = _conv_silu(q_ref, c, n_chunks, cwq_ref, cbq_ref)
            kc = _conv_silu(k_ref, c, n_chunks, cwk_ref, cbk_ref)
            vc = _conv_silu(v_ref, c, n_chunks, cwv_ref, cbv_ref)
            qn = qc * (lax.rsqrt(jnp.sum(qc * qc, axis=-1, keepdims=True) + RMS_EPS) * (GDN_DK ** -0.5))
            kn = kc * lax.rsqrt(jnp.sum(kc * kc, axis=-1, keepdims=True) + RMS_EPS)
            kb = kn.astype(BF16)
            kk = _dot_nt(kb, kb)
            qk = _dot_nt(qn.astype(BF16), kb)
            raw = rows_ref[:, rows]
            beta = jax.nn.sigmoid(raw)
            gl = -jnp.exp(alog_ref[...]) * _softplus(raw + bias_ref[...])
            for d in range(2):
                gcum = _dot(gl, cum[d], precision=HIGHEST)
                cols = _rows_to_cols(beta, gcum)
                for j in range(2):
                    rg = 4 + 2 * d + j
                    b_c = cols[:, 2 * d + j:2 * d + j + 1]
                    g_c = cols[:, 8 + rg:8 + rg + 1]
                    dec = jnp.exp(jnp.where(keep[d], g_c - gcum[rg:rg + 1, :], -jnp.inf))
                    m = kk * dec * b_c
                    eg = jnp.exp(g_c)
                    last = gcum[rg:rg + 1, edge[d]:edge[d] + 1]
                    ch = 2 * d + j
                    qq_ref[ch, rows, :] = jnp.concatenate([qn * eg, qk * dec], axis=1).astype(BF16)
                    kt_ref[ch, rows, :] = (kn * jnp.exp(last - g_c)).T.astype(BF16)
                    el_ref[ch, c] = jnp.broadcast_to(jnp.exp(last), (1, LANES))
                    rhs = jnp.concatenate([vc[:, GDN_DV * j:GDN_DV * (j + 1)] * b_c, kn * (b_c * eg)], axis=1)
                    work.append(dict(ch=ch, rows=rows, mb=m.astype(BF16), rhs=rhs,
                                     tinv=eye - jnp.where(masks[0], m, 0.0)))
        for lvl in range(len(masks) - 1):
            for wk in work:
                tb = wk["tinv"].astype(BF16)
                y = _dot(tb, wk["mb"] * mk_ref[lvl])
                wk["tinv"] = wk["tinv"] - _dot(y.astype(BF16), tb)
        for wk in work:
            rhs = wk["rhs"]
            sol = rhs + _dot((wk["tinv"] - eye).astype(BF16), rhs.astype(BF16))
            u0_ref[wk["ch"], wk["rows"], :] = sol[:, :GDN_DV]
            w_ref[wk["ch"], wk["rows"], :] = sol[:, GDN_DV:].astype(BF16)
        return carry

    def trip(t, second_half):
        cidx = (t, n_chunks - 1 - t)
        rows = [pl.ds(pl.multiple_of(c * CHUNK, CHUNK), CHUNK) for c in cidx]
        st = [st_ref[d, j] for d, j in chains]
        sb = [s.astype(BF16) for s in st]
        ws = [_dot(w_ref[2 * d + j, rows[d], :], sb[2 * d + j]) for d, j in chains]
        ub = [(u0_ref[2 * d + j, rows[d], :] - ws[2 * d + j]).astype(BF16) for d, j in chains]
        outs = [_dot(qq_ref[2 * d + j, rows[d], :], jnp.concatenate([sb[2 * d + j], ub[2 * d + j]], axis=0))
                for d, j in chains]
        new = [st[2 * d + j] * el_ref[2 * d + j, cidx[d]] + _dot(kt_ref[2 * d + j, rows[d], :], ub[2 * d + j])
               for d, j in chains]
        for d, j in chains:
            st_ref[d, j] = new[2 * d + j]
        for d in range(2):
            o = jnp.concatenate([outs[2 * d], outs[2 * d + 1]], axis=1)
            if second_half:
                o_ref[rows[d], :] = (part_ref[rows[d], :] + o).astype(BF16)
            else:
                part_ref[rows[d], :] = o

    def first_body(t, carry):
        trip(t, False)
        return carry

    def second_body(t, carry):
        trip(t, True)
        return carry

    lax.fori_loop(0, n_chunks // GDN_PREP_CHUNKS, prepare, 0)
    lax.fori_loop(0, n_chunks // 2, first_body, 0)
    lax.fori_loop(n_chunks // 2, n_chunks, second_body, 0)
    if emit_state:
        sout_ref[...] = st_ref[...]


def _gdn_scan(p_main, rows, prm, s0, n_seq, seq_len, row_block0, emit_state):
    vw = 2 * GDN_DV
    k0 = GDN_KEY_DIM // GDN_DK
    v0 = (2 * GDN_KEY_DIM) // vw
    has_s0 = s0 is not None
    rb = lambda b: row_block0 + b
    st_block = (None, None, 2, 2, GDN_DK, GDN_DV)
    st_map = lambda b, h: (b, 0, 0, h, 0, 0)
    in_specs = [
        pl.BlockSpec((seq_len, GDN_DK), lambda b, h: (rb(b), h)),
        pl.BlockSpec((seq_len, GDN_DK), lambda b, h: (rb(b), k0 + h)),
        pl.BlockSpec((seq_len, vw), lambda b, h: (rb(b), v0 + h)),
        pl.BlockSpec((None, 8, seq_len), lambda b, h: (h, 0, rb(b))),
        pl.BlockSpec((None, 8, LANES), lambda b, h: (h, 0, 0)),
        pl.BlockSpec((None, 8, LANES), lambda b, h: (h, 0, 0)),
        pl.BlockSpec((3, GDN_DK), lambda b, h: (0, h)),
        pl.BlockSpec((3, GDN_DK), lambda b, h: (0, k0 + h)),
        pl.BlockSpec((3, vw), lambda b, h: (0, v0 + h)),
        pl.BlockSpec((1, GDN_DK), lambda b, h: (0, h)),
        pl.BlockSpec((1, GDN_DK), lambda b, h: (0, k0 + h)),
        pl.BlockSpec((1, vw), lambda b, h: (0, v0 + h)),
    ]
    args = [p_main, p_main, p_main, rows, prm["bias"], prm["alog"],
            prm["conv_w"], prm["conv_w"], prm["conv_w"], prm["conv_b"], prm["conv_b"], prm["conv_b"]]
    if has_s0:
        in_specs.append(pl.BlockSpec(st_block, st_map))
        args.append(s0)
    out_specs = [pl.BlockSpec((seq_len, vw), lambda b, h: (b, h))]
    out_shape = [jax.ShapeDtypeStruct((n_seq * seq_len, GDN_VAL_DIM), BF16)]
    if emit_state:
        out_specs.append(pl.BlockSpec(st_block, st_map))
        out_shape.append(jax.ShapeDtypeStruct((n_seq, 1, 2, GDN_V_HEADS, GDN_DK, GDN_DV), F32))
    assert seq_len % (2 * CHUNK) == 0
    outs = pl.pallas_call(
        functools.partial(_gdn_kernel, n_chunks=seq_len // CHUNK, has_s0=has_s0, emit_state=emit_state),
        grid=(n_seq, GDN_K_HEADS),
        in_specs=in_specs,
        out_specs=out_specs,
        out_shape=out_shape,
        scratch_shapes=[
            pltpu.VMEM((4, seq_len, GDN_DV), F32),
            pltpu.VMEM((4, seq_len, GDN_DK), BF16),
            pltpu.VMEM((4, seq_len, GDN_DK + CHUNK), BF16),
            pltpu.VMEM((4, seq_len, CHUNK), BF16),
            pltpu.VMEM((4, seq_len // CHUNK, 1, LANES), F32),
            pltpu.VMEM((seq_len, vw), F32),
            pltpu.VMEM((CHUNK.bit_length() - 2, CHUNK, CHUNK), BF16),
            pltpu.VMEM((2, 2, GDN_DK, GDN_DV), F32),
        ],
        compiler_params=_cparams(("arbitrary", "arbitrary")),
        name="gdn_scan",
    )(*args)
    return outs if emit_state else (outs[0], None)


def _route(logits_t, rbias):
    aff = jax.nn.sigmoid(logits_t)
    sel = aff + rbias
    a = [aff[e:e + 1, :] for e in range(N_EXPERTS)]
    s = [sel[e:e + 1, :] for e in range(N_EXPERTS)]
    gs = []
    for g in range(N_EXPERT_GROUPS):
        v = s[EXPERTS_PER_GROUP * g:EXPERTS_PER_GROUP * (g + 1)]
        best = None
        for x in range(EXPERTS_PER_GROUP):
            for y in range(x + 1, EXPERTS_PER_GROUP):
                ps = v[x] + v[y]
                best = ps if best is None else jnp.maximum(best, ps)
        gs.append(best)
    top, gi = gs[0], jnp.zeros_like(gs[0], dtype=jnp.int32)
    for g in range(1, N_EXPERT_GROUPS):
        up = gs[g] > top
        top = jnp.where(up, gs[g], top)
        gi = jnp.where(up, g, gi)
    ms = [jnp.where(gi == e // EXPERTS_PER_GROUP, s[e], -jnp.inf) for e in range(N_EXPERTS)]
    b1, i1 = ms[0], jnp.zeros_like(gi)
    for e in range(1, N_EXPERTS):
        up = ms[e] > b1
        b1 = jnp.where(up, ms[e], b1)
        i1 = jnp.where(up, e, i1)
    b2, i2 = jnp.full_like(b1, -jnp.inf), jnp.zeros_like(gi)
    for e in range(N_EXPERTS):
        cand = jnp.where(i1 == e, -jnp.inf, ms[e])
        up = cand > b2
        b2 = jnp.where(up, cand, b2)
        i2 = jnp.where(up, e, i2)
    w1 = sum(jnp.where(i1 == e, a[e], 0.0) for e in range(N_EXPERTS))
    w2 = sum(jnp.where(i2 == e, a[e], 0.0) for e in range(N_EXPERTS))
    tot = w1 + w2
    rows = [jnp.where(i1 == e, w1, 0.0) / tot + jnp.where(i2 == e, w2, 0.0) / tot for e in range(N_EXPERTS)]
    return jnp.concatenate(rows, axis=0)


def _post_kernel(y_ref, z_ref, nw_ref, wo_ref, x_ref, mod_ref, n2_ref, wrt_ref, rb_ref,
                 x1_ref, hf_ref, gate_ref, *, group_width, gate_before_norm):
    y = y_ref[...].astype(F32)
    gz = _silu(z_ref[...].astype(F32))
    if gate_before_norm:
        y = y * gz
    parts = [_rms_unit(y[:, s:s + group_width]) for s in range(0, y.shape[1], group_width)]
    yn = jnp.concatenate(parts, axis=1) * nw_ref[...]
    if not gate_before_norm:
        yn = yn * gz
    out = _dot(yn.astype(BF16), wo_ref[...])
    x1 = x_ref[...] + mod_ref[2:3, :] * out
    x1_ref[...] = x1
    hf = _rms_unit(x1) * n2_ref[...] * (1.0 + mod_ref[4:5, :]) + mod_ref[3:4, :]
    hf_ref[...] = hf.astype(BF16)
    logits_t = _dot_nt(wrt_ref[...], hf, precision=HIGHEST)
    gates_t = _route(logits_t, rb_ref[...])
    pad = jnp.zeros((LANES - N_EXPERTS, gates_t.shape[1]), F32)
    gate_ref[...] = jnp.concatenate([gates_t, pad], axis=0).T


def _post(y, p_main, z_block0, x, mods, layer, seg_of, norm_w_full, w_out, norm2_w, w_router_t, router_bias,
          group_width, gate_before_norm):
    t, d = x.shape
    inner = y.shape[1]
    tm = seg_of.tile
    return pl.pallas_call(
        functools.partial(_post_kernel, group_width=group_width, gate_before_norm=gate_before_norm),
        grid=(t // tm,),
        in_specs=[
            pl.BlockSpec((tm, inner), lambda i: (i, 0)),
            pl.BlockSpec((tm, inner), lambda i: (i, z_block0)),
            pl.BlockSpec((1, inner), lambda i: (0, 0)),
            pl.BlockSpec((inner, d), lambda i: (0, 0)),
            pl.BlockSpec((tm, d), lambda i: (i, 0)),
            pl.BlockSpec((None, None, 6, d), lambda i: (layer, seg_of(i), 0, 0)),
            pl.BlockSpec((1, d), lambda i: (0, 0)),
            pl.BlockSpec((N_EXPERTS, d), lambda i: (0, 0)),
            pl.BlockSpec((N_EXPERTS, 1), lambda i: (0, 0)),
        ],
        out_specs=[
            pl.BlockSpec((tm, d), lambda i: (i, 0)),
            pl.BlockSpec((tm, d), lambda i: (i, 0)),
            pl.BlockSpec((tm, LANES), lambda i: (i, 0)),
        ],
        out_shape=[
            jax.ShapeDtypeStruct((t, d), F32),
            jax.ShapeDtypeStruct((t, d), BF16),
            jax.ShapeDtypeStruct((t, LANES), F32),
        ],
        compiler_params=_cparams(("arbitrary",)),
        name="post",
    )(y, p_main, norm_w_full.reshape(1, inner), w_out, x, mods, norm2_w.reshape(1, d), w_router_t,
      router_bias.reshape(N_EXPERTS, 1))


def _moe_kernel(hf_ref, gate_ref, wg_ref, wu_ref, wd_ref, x1_ref, mod_ref, fn_ref, o_ref, acc_ref, *, final_norm):
    e = pl.program_id(1)

    @pl.when(e == 0)
    def _():
        acc_ref[...] = jnp.zeros(acc_ref.shape, F32)

    h = hf_ref[...]
    hid = _silu(_dot(h, wg_ref[...])) * _dot(h, wu_ref[...])
    y = _dot(hid.astype(BF16), wd_ref[...])
    lane = lax.broadcasted_iota(jnp.int32, gate_ref.shape, 1)
    gcol = jnp.sum(jnp.where(lane == e, gate_ref[...], 0.0), axis=1, keepdims=True)
    acc_ref[...] += y * gcol

    @pl.when(e == pl.num_programs(1) - 1)
    def _():
        x2 = x1_ref[...] + mod_ref[5:6, :] * acc_ref[...]
        if final_norm:
            x2 = _rms_unit(x2) * fn_ref[...]
        o_ref[...] = x2


def _moe(hf, gates, wg, wu, wd, x1, mods, layer, seg_of, final_norm_w, final_norm):
    t, d = x1.shape
    n_e, _, de = wg.shape
    tm = seg_of.tile
    return pl.pallas_call(
        functools.partial(_moe_kernel, final_norm=final_norm),
        grid=(t // tm, n_e),
        in_specs=[
            pl.BlockSpec((tm, d), lambda i, e: (i, 0)),
            pl.BlockSpec((tm, LANES), lambda i, e: (i, 0)),
            pl.BlockSpec((None, d, de), lambda i, e: (e, 0, 0)),
            pl.BlockSpec((None, d, de), lambda i, e: (e, 0, 0)),
            pl.BlockSpec((None, de, d), lambda i, e: (e, 0, 0)),
            pl.BlockSpec((tm, d), lambda i, e: (i, 0)),
            pl.BlockSpec((None, None, 6, d), lambda i, e: (layer, seg_of(i), 0, 0)),
            pl.BlockSpec((1, d), lambda i, e: (0, 0)),
        ],
        out_specs=pl.BlockSpec((tm, d), lambda i, e: (i, 0)),
        out_shape=jax.ShapeDtypeStruct((t, d), F32),
        scratch_shapes=[pltpu.VMEM((tm, d), F32)],
        compiler_params=_cparams(("arbitrary", "arbitrary")),
        name="moe",
    )(hf, gates, wg, wu, wd, x1, mods, final_norm_w.reshape(1, d))


def _lanes(v):
    return jnp.broadcast_to(v[..., None].astype(F32), v.shape + (LANES,))


def _ssd_small_order():
    idx = []
    for g in range(SSD_GROUPS):
        for d in range(2):
            for h in range(SSD_HEADS_PER_GROUP):
                idx.append(d * SSD_HEADS + SSD_HEADS_PER_GROUP * g + h)
    return jnp.array(idx, jnp.int32)


def _gdn_small_order():
    idx = []
    for kh in range(GDN_K_HEADS):
        for which in range(2):
            for d in range(2):
                for j in range(2):
                    idx.append(d * 2 * GDN_V_HEADS + which * GDN_V_HEADS + 2 * kh + j)
    return jnp.array(idx, jnp.int32)


def _ssd_params(j, ssd_w_in, ssd_conv_w, ssd_conv_b, ssd_a_log, ssd_dt_bias, ssd_d):
    order = _ssd_small_order()
    w = ssd_w_in[j]
    per_row = lambda v: v.reshape(-1)[order].reshape(SSD_GROUPS, 2 * SSD_HEADS_PER_GROUP)
    return dict(
        w_main=w[:, :SSD_MAIN].astype(BF16),
        w_small_t=w[:, SSD_MAIN:][:, order].T.astype(BF16),
        conv_w=ssd_conv_w[j], conv_b=ssd_conv_b[j].reshape(1, -1),
        bias=_lanes(per_row(ssd_dt_bias[j])), alog=_lanes(per_row(ssd_a_log[j])),
        dskip=jnp.repeat(ssd_d[j].astype(F32), SSD_HEADDIM).reshape(1, SSD_INNER),
    )


def _gdn_params(j, gdn_w_in, gdn_conv_w, gdn_conv_b, gdn_a_log, gdn_dt_bias):
    order = _gdn_small_order()
    w = gdn_w_in[j]
    def per_row(v):
        full = jnp.stack([jnp.zeros_like(v), v], axis=1).astype(F32)
        return full.reshape(-1)[order].reshape(GDN_K_HEADS, 8)
    return dict(
        w_main=w[:, :GDN_MAIN].astype(BF16),
        w_small_t=w[:, GDN_MAIN:][:, order].T.astype(BF16),
        conv_w=gdn_conv_w[j], conv_b=gdn_conv_b[j].reshape(1, -1),
        bias=_lanes(per_row(gdn_dt_bias[j])), alog=_lanes(per_row(gdn_a_log[j])),
    )


def _ssd_state_in(s):
    n = s.shape[0]
    s = s.reshape(n, 2, SSD_GROUPS, SSD_PAIRS_PER_GROUP, 2, SSD_HEADDIM, SSD_STATE)
    return s.transpose(0, 1, 2, 3, 6, 4, 5).reshape(n, 2, SSD_GROUPS, SSD_PAIRS_PER_GROUP, SSD_STATE, LANES)


def _ssd_state_out(s):
    n = s.shape[0]
    s = s.reshape(n, 2, SSD_GROUPS, SSD_PAIRS_PER_GROUP, SSD_STATE, 2, SSD_HEADDIM)
    return s.transpose(0, 1, 2, 3, 5, 6, 4).reshape(n, 2, SSD_HEADS, SSD_HEADDIM, SSD_STATE)


def kernel(x_prompt, x_sample, state_ssd, state_gdn, c, c_ctx, w_ada, b_ada, norm_w, final_norm_w, ssd_w_in, ssd_conv_w, ssd_conv_b, ssd_a_log, ssd_dt_bias, ssd_d, ssd_norm_w, ssd_w_out, gdn_w_in, gdn_conv_w, gdn_conv_b, gdn_a_log, gdn_dt_bias, gdn_norm_w, gdn_w_out, w_router, router_bias, moe_w_gate, moe_w_up, moe_w_down):
    nb, lc, d = x_prompt.shape
    ns, ls, _ = x_sample.shape
    depth = w_ada.shape[0]
    tc, ts = nb * lc, ns * ls
    assert tc % ls == 0 and ns + 1 <= N_MOD_ROWS

    x = _embed(x_prompt, x_sample)
    cond = jnp.concatenate([c_ctx[None, :], c, jnp.zeros((N_MOD_ROWS - 1 - ns, d), F32)], axis=0)
    mods = _ada(cond, w_ada, b_ada).reshape(depth, N_MOD_ROWS, 6, d)

    seg_big = _SegOf(_tile(1024, tc, ls), tc, ls)
    seg_mid = _SegOf(_tile(512, tc, ls), tc, ls)
    w_router_t = w_router.T.astype(F32)
    ssd_states, gdn_states = [], []
    for i in range(depth):
        j = i // 2
        if i % 2 == 0:
            prm = _ssd_params(j, ssd_w_in, ssd_conv_w, ssd_conv_b, ssd_a_log, ssd_dt_bias, ssd_d)
            p_main, p_small_t = _proj(x, mods, i, seg_big, norm_w[i, 0], prm["w_main"], prm["w_small_t"])
            rows = p_small_t.reshape(SSD_GROUPS, 2 * SSD_HEADS_PER_GROUP, tc + ts)
            y_ctx, s_ctx = _ssd_scan(p_main, rows, prm, None, nb, lc, 0, True)
            y_smp, _ = _ssd_scan(p_main, rows, prm, _ssd_state_in(state_ssd[:, j].astype(F32)), ns, ls,
                                 tc // ls, False)
            ssd_states.append(_ssd_state_out(s_ctx))
            y = jnp.concatenate([y_ctx, y_smp], axis=0)
            x, hf, gates = _post(y, p_main, 0, x, mods, i, seg_mid, ssd_norm_w[j], ssd_w_out[j].astype(BF16),
                                 norm_w[i, 1], w_router_t, router_bias, SSD_GROUP_WIDTH, True)
        else:
            prm = _gdn_params(j, gdn_w_in, gdn_conv_w, gdn_conv_b, gdn_a_log, gdn_dt_bias)
            p_main, p_small_t = _proj(x, mods, i, seg_big, norm_w[i, 0], prm["w_main"], prm["w_small_t"])
            rows = p_small_t.reshape(GDN_K_HEADS, 8, tc + ts)
            o_ctx, s_ctx = _gdn_scan(p_main, rows, prm, None, nb, lc, 0, True)
            o_smp, _ = _gdn_scan(p_main, rows, prm, state_gdn[:, j:j + 1].astype(F32), ns, ls, tc // ls, False)
            gdn_states.append(s_ctx[:, 0])
            y = jnp.concatenate([o_ctx, o_smp], axis=0)
            x, hf, gates = _post(y, p_main, GDN_CONV_DIM // GDN_VAL_DIM, x, mods, i, seg_mid,
                                 jnp.tile(gdn_norm_w[j], GDN_V_HEADS), gdn_w_out[j].astype(BF16),
                                 norm_w[i, 1], w_router_t, router_bias, GDN_DV, False)
        x = _moe(hf, gates, moe_w_gate[i].astype(BF16), moe_w_up[i].astype(BF16), moe_w_down[i].astype(BF16),
                 x, mods, i, seg_big, final_norm_w, i == depth - 1)
    y_prompt = x[:tc].reshape(nb, lc, d)
    y_sample = x[tc:].reshape(ns, ls, d)
    return (y_prompt, y_sample, jnp.stack(ssd_states, axis=1), jnp.stack(gdn_states, axis=1))
```

```python
import functools

import jax
import jax.numpy as jnp
from jax import lax
from jax.experimental import pallas as pl
from jax.experimental.pallas import tpu as pltpu

F32 = jnp.float32
BF16 = jnp.bfloat16
HIGHEST = lax.Precision.HIGHEST

D_MODEL = 1024
GRID_W = 64
RMS_EPS = 1e-6

SSD_INNER = 2 * D_MODEL
SSD_HEADDIM = 64
SSD_HEADS = SSD_INNER // SSD_HEADDIM
SSD_GROUPS = 4
SSD_STATE = 128
SSD_HEADS_PER_GROUP = SSD_HEADS // SSD_GROUPS
SSD_PAIRS_PER_GROUP = SSD_HEADS_PER_GROUP // 2
SSD_GROUP_WIDTH = SSD_INNER // SSD_GROUPS
SSD_CONV_DIM = SSD_INNER + 2 * SSD_GROUPS * SSD_STATE
SSD_MAIN = SSD_INNER + SSD_CONV_DIM

GDN_K_HEADS = 8
GDN_V_HEADS = 16
GDN_DK = 128
GDN_DV = 128
GDN_KEY_DIM = GDN_K_HEADS * GDN_DK
GDN_VAL_DIM = GDN_V_HEADS * GDN_DV
GDN_CONV_DIM = 2 * GDN_KEY_DIM + GDN_VAL_DIM
GDN_MAIN = GDN_CONV_DIM + GDN_VAL_DIM

N_EXPERTS = 16
EXPERTS_PER_GROUP = 4
N_EXPERT_GROUPS = N_EXPERTS // EXPERTS_PER_GROUP
D_EXPERT = 512

LANES = 128
CHUNK = 128
GDN_PREP_CHUNKS = 2
MOE_ROWS = 160
assert GDN_DK == CHUNK
BF16_ROWS = 16
N_MOD_ROWS = 16
VMEM_LIMIT = 48 * 1024 * 1024


def _cparams(sem):
    return pltpu.CompilerParams(dimension_semantics=sem, vmem_limit_bytes=VMEM_LIMIT)


def _tile(target, *dims):
    t = target
    while any(d % t for d in dims):
        t //= 2
    assert t >= 8
    return t


def _silu(x):
    return x * jax.nn.sigmoid(x)


def _softplus(x):
    return jnp.maximum(x, 0.0) + jnp.log1p(jnp.exp(-jnp.abs(x)))


def _rms_unit(x):
    return x * lax.rsqrt(jnp.mean(x * x, axis=-1, keepdims=True) + RMS_EPS)


def _dot(a, b, **kw):
    return jnp.dot(a, b, preferred_element_type=F32, **kw)


def _dot_nt(a, b, **kw):
    return lax.dot_general(a, b, (((1,), (1,)), ((), ())), preferred_element_type=F32, **kw)


def _dot_tn(a, b):
    return lax.dot_general(a, b, (((0,), (0,)), ((), ())), preferred_element_type=F32)


def _embed_kernel(xp_ref, xs_ref, pos_ref, o_ref, *, n_ctx_tiles):
    i = pl.program_id(0)

    @pl.when(i < n_ctx_tiles)
    def _():
        o_ref[...] = xp_ref[...]

    @pl.when(i >= n_ctx_tiles)
    def _():
        o_ref[...] = xs_ref[...] + pos_ref[...]


def _grid_pos_embed(n_tokens, d):
    t = jnp.arange(n_tokens)
    row = (t // GRID_W).astype(F32)
    col = (t % GRID_W).astype(F32)
    quarter = d // 4
    omega = 1.0 / (10000.0 ** (jnp.arange(quarter, dtype=F32) / quarter))

    def emb(pos):
        ang = pos[:, None] * omega[None, :]
        return jnp.concatenate([jnp.sin(ang), jnp.cos(ang)], axis=-1)

    return jnp.concatenate([emb(row), emb(col)], axis=-1)


def _embed(x_prompt, x_sample):
    nb, lc, d = x_prompt.shape
    ns, ls, _ = x_sample.shape
    tc, ts = nb * lc, ns * ls
    tm = _tile(1024, tc, ls)
    nct, tiles_per_seq = tc // tm, ls // tm
    pos = _grid_pos_embed((ls // GRID_W) * GRID_W, d)
    return pl.pallas_call(
        functools.partial(_embed_kernel, n_ctx_tiles=nct),
        grid=((tc + ts) // tm,),
        in_specs=[
            pl.BlockSpec((tm, d), lambda i: (jnp.minimum(i, nct - 1), 0)),
            pl.BlockSpec((tm, d), lambda i: (jnp.maximum(i - nct, 0), 0)),
            pl.BlockSpec((tm, d), lambda i: (jnp.maximum(i - nct, 0) % tiles_per_seq, 0)),
        ],
        out_specs=pl.BlockSpec((tm, d), lambda i: (i, 0)),
        out_shape=jax.ShapeDtypeStruct((tc + ts, d), F32),
        compiler_params=_cparams(("arbitrary",)),
        name="embed",
    )(x_prompt.reshape(tc, d), x_sample.reshape(ts, d), pos)


def _ada_kernel(c_ref, w_ref, b_ref, o_ref):
    s = _silu(c_ref[...])
    o_ref[...] = _dot(s, w_ref[...], precision=HIGHEST) + b_ref[...]


def _ada(cond, w_ada, b_ada):
    depth, d, six_d = w_ada.shape
    return pl.pallas_call(
        _ada_kernel,
        grid=(depth, six_d // d),
        in_specs=[
            pl.BlockSpec((N_MOD_ROWS, d), lambda l, j: (0, 0)),
            pl.BlockSpec((None, d, d), lambda l, j: (l, 0, j)),
            pl.BlockSpec((None, 1, d), lambda l, j: (l, 0, j)),
        ],
        out_specs=pl.BlockSpec((None, N_MOD_ROWS, d), lambda l, j: (l, 0, j)),
        out_shape=jax.ShapeDtypeStruct((depth, N_MOD_ROWS, six_d), F32),
        compiler_params=_cparams(("arbitrary", "arbitrary")),
        name="ada",
    )(cond, w_ada, b_ada.reshape(depth, 1, six_d))


def _proj_kernel(x_ref, mod_ref, nw_ref, w_ref, wst_ref, o_ref, ost_ref, hm_ref):
    @pl.when(pl.program_id(1) == 0)
    def _():
        hm = _rms_unit(x_ref[...]) * nw_ref[...] * (1.0 + mod_ref[1:2, :]) + mod_ref[0:1, :]
        hmb = hm.astype(BF16)
        hm_ref[...] = hmb
        ost_ref[...] = _dot_nt(wst_ref[...], hmb)

    o_ref[...] = _dot(hm_ref[...], w_ref[...]).astype(BF16)


def _proj(x, mods, layer, seg_of, norm_w, w_main, w_small_t):
    t, d = x.shape
    n_main = w_main.shape[1]
    n_small = w_small_t.shape[0]
    tm = seg_of.tile
    tn = _tile(1024, n_main)
    return pl.pallas_call(
        _proj_kernel,
        grid=(t // tm, n_main // tn),
        in_specs=[
            pl.BlockSpec((tm, d), lambda i, j: (i, 0)),
            pl.BlockSpec((None, None, 6, d), lambda i, j: (layer, seg_of(i), 0, 0)),
            pl.BlockSpec((1, d), lambda i, j: (0, 0)),
            pl.BlockSpec((d, tn), lambda i, j: (0, j)),
            pl.BlockSpec((n_small, d), lambda i, j: (0, 0)),
        ],
        out_specs=[
            pl.BlockSpec((tm, tn), lambda i, j: (i, j)),
            pl.BlockSpec((n_small, tm), lambda i, j: (0, i)),
        ],
        out_shape=[
            jax.ShapeDtypeStruct((t, n_main), BF16),
            jax.ShapeDtypeStruct((n_small, t), F32),
        ],
        scratch_shapes=[pltpu.VMEM((tm, d), BF16)],
        compiler_params=_cparams(("arbitrary", "arbitrary")),
        name="proj",
    )(x, mods, norm_w.reshape(1, d), w_main, w_small_t)


class _SegOf:
    def __init__(self, tile, n_ctx_tokens, sample_len):
        assert n_ctx_tokens % tile == 0 and sample_len % tile == 0
        self.tile = tile
        self.n_ctx_tiles = n_ctx_tokens // tile
        self.tiles_per_sample = sample_len // tile

    def __call__(self, i):
        return jnp.where(i < self.n_ctx_tiles, 0, 1 + (i - self.n_ctx_tiles) // self.tiles_per_sample)


def _chunk_consts():
    ii = lax.broadcasted_iota(jnp.int32, (CHUNK, CHUNK), 0)
    jj = lax.broadcasted_iota(jnp.int32, (CHUNK, CHUNK), 1)
    return ii, jj


def _conv_silu(src_ref, c, n_chunks, w_ref, b_ref):
    seq_len = n_chunks * CHUNK
    r0 = pl.multiple_of(c * CHUNK, CHUNK)
    xb = src_ref[pl.ds(r0, CHUNK), :].astype(F32)
    lo = pl.multiple_of(jnp.maximum(r0 - BF16_ROWS, 0), BF16_ROWS)
    hi = pl.multiple_of(jnp.minimum(r0 + CHUNK, seq_len - BF16_ROWS), BF16_ROWS)
    prev_row = src_ref[pl.ds(lo, BF16_ROWS), :].astype(F32)[BF16_ROWS - 1:BF16_ROWS, :]
    next_row = src_ref[pl.ds(hi, BF16_ROWS), :].astype(F32)[0:1, :]
    prev_row = prev_row * jnp.where(c > 0, 1.0, 0.0)
    next_row = next_row * jnp.where(c < n_chunks - 1, 1.0, 0.0)
    row = lax.broadcasted_iota(jnp.int32, xb.shape, 0)
    xp = jnp.where(row == 0, prev_row, pltpu.roll(xb, 1, 0))
    xn = jnp.where(row == CHUNK - 1, next_row, pltpu.roll(xb, CHUNK - 1, 0))
    y = w_ref[0:1, :] * xp + w_ref[1:2, :] * xb + w_ref[2:3, :] * xn + b_ref[...]
    return _silu(y)


def _rows_to_cols(*row_blocks):
    n = sum(b.shape[0] for b in row_blocks)
    pad = jnp.zeros((LANES - n, CHUNK), F32)
    return jnp.concatenate(list(row_blocks) + [pad], axis=0).T


def _ssd_kernel(*refs, n_chunks, has_s0, has_prev, emit_state):
    it = iter(refs)
    x_ref, b_ref, c_ref, rows_ref, bias_ref, alog_ref = (next(it) for _ in range(6))
    cwx_ref, cwb_ref, cwc_ref, cbx_ref, cbb_ref, cbc_ref, dsk_ref = (next(it) for _ in range(7))
    s0_ref = next(it) if has_s0 else None
    if has_prev:
        next(it)
    y_ref = next(it)
    sout_ref = next(it) if emit_state else None
    xc_ref, cc_ref, bt_ref, g_ref, cols_ref, arow_ref, lrow_ref, part_ref, st_ref = (next(it) for _ in range(9))

    ii, jj = _chunk_consts()
    lane = lax.broadcasted_iota(jnp.int32, (CHUNK, LANES), 1)
    first_head = lane < SSD_HEADDIM
    nh = SSD_HEADS_PER_GROUP
    keep = ((ii >= jj), (ii <= jj))
    cum = ((ii <= jj).astype(F32), (ii >= jj).astype(F32))
    edge = (CHUNK - 1, 0)

    if has_s0:
        st_ref[...] = s0_ref[...]
    else:
        st_ref[...] = jnp.zeros(st_ref.shape, F32)

    def prepare(c, carry):
        rows = pl.ds(pl.multiple_of(c * CHUNK, CHUNK), CHUNK)
        xc_ref[rows, :] = _conv_silu(x_ref, c, n_chunks, cwx_ref, cbx_ref).astype(BF16)
        bc = _conv_silu(b_ref, c, n_chunks, cwb_ref, cbb_ref)
        cb = _conv_silu(c_ref, c, n_chunks, cwc_ref, cbc_ref).astype(BF16)
        cc_ref[rows, :] = cb
        bt_ref[rows, :] = bc.T.astype(BF16)
        g_ref[rows, :] = _dot_nt(cb, bc.astype(BF16))
        for d in range(2):
            sl = slice(nh * d, nh * d + nh)
            dt = _softplus(rows_ref[sl, rows] + bias_ref[sl, :])
            acs = _dot(dt * (-jnp.exp(alog_ref[sl, :])), cum[d], precision=HIGHEST)
            arow_ref[d, :, rows] = acs
            lrow_ref[d, :, rows] = jnp.log(dt) - acs
            cols_ref[d, rows, :] = _rows_to_cols(acs, dt)
        return carry

    def trip(t, second_half):
        cidx = (t, n_chunks - 1 - t)
        rows = [pl.ds(pl.multiple_of(c * CHUNK, CHUNK), CHUNK) for c in cidx]
        chains = [(d, p) for d in range(2) for p in range(SSD_PAIRS_PER_GROUP)]
        st = {ch: st_ref[ch[0], ch[1]] for ch in chains}
        xcb = [xc_ref[r, :] for r in rows]
        cb = [cc_ref[r, :] for r in rows]
        bt = [bt_ref[r, :] for r in rows]
        g = [g_ref[r, :] for r in rows]
        cols = [cols_ref[d, rows[d], :] for d in range(2)]
        arow = [arow_ref[d, :, rows[d]] for d in range(2)]
        lrow = [lrow_ref[d, :, rows[d]] for d in range(2)]
        zero = jnp.zeros((CHUNK, LANES), BF16)
        intra, inter = {}, {}
        for d, p in chains:
            xpb = xcb[d][:, LANES * p:LANES * (p + 1)]
            ms = []
            for k in range(2):
                h = 2 * p + k
                decay = jnp.exp(jnp.where(keep[d], cols[d][:, h:h + 1] + lrow[d][h:h + 1, :], -jnp.inf))
                ms.append((g[d] * decay).astype(BF16))
            xblk = jnp.concatenate([jnp.where(first_head, xpb, zero), jnp.where(first_head, zero, xpb)], axis=0)
            intra[(d, p)] = _dot(jnp.concatenate(ms, axis=1), xblk)
            inter[(d, p)] = _dot(cb[d], st[(d, p)].astype(BF16))
        ys, new = {}, {}
        for d, p in chains:
            h0, h1 = 2 * p, 2 * p + 1
            acs_c = jnp.where(first_head, cols[d][:, h0:h0 + 1], cols[d][:, h1:h1 + 1])
            dt_c = jnp.where(first_head, cols[d][:, nh + h0:nh + h0 + 1], cols[d][:, nh + h1:nh + h1 + 1])
            last = jnp.where(first_head[0:1, :], arow[d][h0:h0 + 1, edge[d]:edge[d] + 1],
                             arow[d][h1:h1 + 1, edge[d]:edge[d] + 1])
            ys[(d, p)] = intra[(d, p)] + inter[(d, p)] * jnp.exp(acs_c)
            xpair = xcb[d][:, LANES * p:LANES * (p + 1)].astype(F32)
            xt = (xpair * (jnp.exp(last - acs_c) * dt_c)).astype(BF16)
            new[(d, p)] = st[(d, p)] * jnp.exp(last) + _dot(bt[d], xt)
        for d, p in chains:
            st_ref[d, p] = new[(d, p)]
        for d in range(2):
            y = jnp.concatenate([ys[(d, p)] for p in range(SSD_PAIRS_PER_GROUP)], axis=1)
            if second_half:
                skip = dsk_ref[...] * xcb[d].astype(F32)
                y_ref[rows[d], :] = (part_ref[rows[d], :] + y + skip).astype(BF16)
            else:
                part_ref[rows[d], :] = y

    def first_body(t, carry):
        trip(t, False)
        return carry

    def second_body(t, carry):
        trip(t, True)
        return carry

    lax.fori_loop(0, n_chunks, prepare, 0)
    lax.fori_loop(0, n_chunks // 2, first_body, 0)
    lax.fori_loop(n_chunks // 2, n_chunks, second_body, 0)
    if emit_state:
        sout_ref[...] = st_ref[...]


def _ssd_scan(p_main, rows, prm, s0, n_seq, seq_len, row_block0, emit_state, y_prev=None):
    t = p_main.shape[0]
    gw = SSD_GROUP_WIDTH
    x0 = SSD_INNER // gw
    b0 = (2 * SSD_INNER) // SSD_STATE
    c0 = b0 + SSD_GROUPS
    cx0 = 0
    cb0 = SSD_INNER // SSD_STATE
    cc0 = cb0 + SSD_GROUPS
    has_s0 = s0 is not None
    rb = lambda b: row_block0 + b
    st_block = (None, 2, None, SSD_PAIRS_PER_GROUP, SSD_STATE, LANES)
    st_map = lambda b, g: (b, 0, g, 0, 0, 0)
    in_specs = [
        pl.BlockSpec((seq_len, gw), lambda b, g: (rb(b), x0 + g)),
        pl.BlockSpec((seq_len, SSD_STATE), lambda b, g: (rb(b), b0 + g)),
        pl.BlockSpec((seq_len, SSD_STATE), lambda b, g: (rb(b), c0 + g)),
        pl.BlockSpec((None, 2 * SSD_HEADS_PER_GROUP, seq_len), lambda b, g: (g, 0, rb(b))),
        pl.BlockSpec((None, 2 * SSD_HEADS_PER_GROUP, LANES), lambda b, g: (g, 0, 0)),
        pl.BlockSpec((None, 2 * SSD_HEADS_PER_GROUP, LANES), lambda b, g: (g, 0, 0)),
        pl.BlockSpec((3, gw), lambda b, g: (0, cx0 + g)),
        pl.BlockSpec((3, SSD_STATE), lambda b, g: (0, cb0 + g)),
        pl.BlockSpec((3, SSD_STATE), lambda b, g: (0, cc0 + g)),
        pl.BlockSpec((1, gw), lambda b, g: (0, cx0 + g)),
        pl.BlockSpec((1, SSD_STATE), lambda b, g: (0, cb0 + g)),
        pl.BlockSpec((1, SSD_STATE), lambda b, g: (0, cc0 + g)),
        pl.BlockSpec((1, gw), lambda b, g: (0, g)),
    ]
    args = [p_main, p_main, p_main, rows, prm["bias"], prm["alog"],
            prm["conv_w"], prm["conv_w"], prm["conv_w"], prm["conv_b"], prm["conv_b"], prm["conv_b"], prm["dskip"]]
    if has_s0:
        in_specs.append(pl.BlockSpec(st_block, st_map))
        args.append(s0)
    aliases = {}
    if y_prev is not None:
        in_specs.append(pl.BlockSpec(memory_space=pl.ANY))
        args.append(y_prev)
        aliases = {len(args) - 1: 0}
    out_specs = [pl.BlockSpec((seq_len, gw), lambda b, g: (rb(b), g))]
    out_shape = [jax.ShapeDtypeStruct((t, SSD_INNER), BF16)]
    if emit_state:
        out_specs.append(pl.BlockSpec(st_block, st_map))
        out_shape.append(jax.ShapeDtypeStruct(
            (n_seq, 2, SSD_GROUPS, SSD_PAIRS_PER_GROUP, SSD_STATE, LANES), F32))
    assert t % seq_len == 0 and seq_len % (2 * CHUNK) == 0 and SSD_STATE == CHUNK
    outs = pl.pallas_call(
        functools.partial(_ssd_kernel, n_chunks=seq_len // CHUNK, has_s0=has_s0, has_prev=y_prev is not None,
                          emit_state=emit_state),
        grid=(n_seq, SSD_GROUPS),
        in_specs=in_specs,
        out_specs=out_specs,
        out_shape=out_shape,
        input_output_aliases=aliases,
        scratch_shapes=[
            pltpu.VMEM((seq_len, gw), BF16),
            pltpu.VMEM((seq_len, SSD_STATE), BF16),
            pltpu.VMEM((seq_len, CHUNK), BF16),
            pltpu.VMEM((seq_len, CHUNK), F32),
            pltpu.VMEM((2, seq_len, LANES), F32),
            pltpu.VMEM((2, SSD_HEADS_PER_GROUP, seq_len), F32),
            pltpu.VMEM((2, SSD_HEADS_PER_GROUP, seq_len), F32),
            pltpu.VMEM((seq_len, gw), F32),
            pltpu.VMEM((2, SSD_PAIRS_PER_GROUP, SSD_STATE, LANES), F32),
        ],
        compiler_params=_cparams(("arbitrary", "arbitrary")),
        name="ssd_scan",
    )(*args)
    return outs if emit_state else (outs[0], None)


def _merge_masks(ii, jj):
    masks = []
    s = 1
    while s < CHUNK:
        masks.append(((ii // s) != (jj // s)) & ((ii // (2 * s)) == (jj // (2 * s))))
        s *= 2
    return masks


def _gdn_kernel(*refs, n_chunks, has_s0, has_prev, emit_state):
    it = iter(refs)
    q_ref, k_ref, v_ref, rows_ref, bias_ref, alog_ref = (next(it) for _ in range(6))
    cwq_ref, cwk_ref, cwv_ref, cbq_ref, cbk_ref, cbv_ref = (next(it) for _ in range(6))
    s0_ref = next(it) if has_s0 else None
    if has_prev:
        next(it)
    o_ref = next(it)
    sout_ref = next(it) if emit_state else None
    u0_ref, w_ref, qq_ref, kt_ref, el_ref, part_ref, mk_ref, st_ref = (next(it) for _ in range(8))

    ii, jj = _chunk_consts()
    eye = (ii == jj).astype(F32)
    masks = _merge_masks(ii, jj)
    for lvl, mask in enumerate(masks[1:]):
        mk_ref[lvl] = mask.astype(BF16)
    keep = ((ii >= jj), (ii <= jj))
    cum = ((ii <= jj).astype(F32), (ii >= jj).astype(F32))
    edge = (CHUNK - 1, 0)
    chains = [(d, j) for d in range(2) for j in range(2)]

    if has_s0:
        st_ref[...] = s0_ref[...]
    else:
        st_ref[...] = jnp.zeros(st_ref.shape, F32)

    def prepare(p, carry):
        work = []
        for cc in range(GDN_PREP_CHUNKS):
            c = p * GDN_PREP_CHUNKS + cc
            rows = pl.ds(pl.multiple_of(c * CHUNK, CHUNK), CHUNK)
            qc = _conv_silu(q_ref, c, n_chunks, cwq_ref, cbq_ref)
            kc = _conv_silu(k_ref, c, n_chunks, cwk_ref, cbk_ref)
            vc = _conv_silu(v_ref, c, n_chunks, cwv_ref, cbv_ref)
            qn = qc * (lax.rsqrt(jnp.sum(qc * qc, axis=-1, keepdims=True) + RMS_EPS) * (GDN_DK ** -0.5))
            kn = kc * lax.rsqrt(jnp.sum(kc * kc, axis=-1, keepdims=True) + RMS_EPS)
            kb = kn.astype(BF16)
            kk = _dot_nt(kb, kb)
            qk = _dot_nt(qn.astype(BF16), kb)
            raw = rows_ref[:, rows]
            beta = jax.nn.sigmoid(raw)
            gl = -jnp.exp(alog_ref[...]) * _softplus(raw + bias_ref[...])
            for d in range(2):
                gcum = _dot(gl, cum[d], precision=HIGHEST)
                cols = _rows_to_cols(beta, gcum)
                for j in range(2):
                    rg = 4 + 2 * d + j
                    b_c = cols[:, 2 * d + j:2 * d + j + 1]
                    g_c = cols[:, 8 + rg:8 + rg + 1]
                    dec = jnp.exp(jnp.where(keep[d], g_c - gcum[rg:rg + 1, :], -jnp.inf))
                    m = kk * dec * b_c
                    eg = jnp.exp(g_c)
                    last = gcum[rg:rg + 1, edge[d]:edge[d] + 1]
                    ch = 2 * d + j
                    qq_ref[ch, rows, :] = jnp.concatenate([qn * eg, qk * dec], axis=1).astype(BF16)
                    kt_ref[ch, rows, :] = (kn * jnp.exp(last - g_c)).T.astype(BF16)
                    el_ref[ch, c] = jnp.broadcast_to(jnp.exp(last), (1, LANES))
                    rhs = jnp.concatenate([vc[:, GDN_DV * j:GDN_DV * (j + 1)] * b_c, kn * (b_c * eg)], axis=1)
                    work.append(dict(ch=ch, rows=rows, mb=m.astype(BF16), rhs=rhs,
                                     tinv=eye - jnp.where(masks[0], m, 0.0)))
        for lvl in range(len(masks) - 1):
            for wk in work:
                tb = wk["tinv"].astype(BF16)
                y = _dot(tb, wk["mb"] * mk_ref[lvl])
                wk["tinv"] = wk["tinv"] - _dot(y.astype(BF16), tb)
        for wk in work:
            rhs = wk["rhs"]
            sol = rhs + _dot((wk["tinv"] - eye).astype(BF16), rhs.astype(BF16))
            u0_ref[wk["ch"], wk["rows"], :] = sol[:, :GDN_DV]
            w_ref[wk["ch"], wk["rows"], :] = sol[:, GDN_DV:].astype(BF16)
        return carry

    def trip(t, second_half):
        cidx = (t, n_chunks - 1 - t)
        rows = [pl.ds(pl.multiple_of(c * CHUNK, CHUNK), CHUNK) for c in cidx]
        st = [st_ref[d, j] for d, j in chains]
        sb = [s.astype(BF16) for s in st]
        ws = [_dot(w_ref[2 * d + j, rows[d], :], sb[2 * d + j]) for d, j in chains]
        ub = [(u0_ref[2 * d + j, rows[d], :] - ws[2 * d + j]).astype(BF16) for d, j in chains]
        outs = [_dot(qq_ref[2 * d + j, rows[d], :], jnp.concatenate([sb[2 * d + j], ub[2 * d + j]], axis=0))
                for d, j in chains]
        new = [st[2 * d + j] * el_ref[2 * d + j, cidx[d]] + _dot(kt_ref[2 * d + j, rows[d], :], ub[2 * d + j])
               for d, j in chains]
        for d, j in chains:
            st_ref[d, j] = new[2 * d + j]
        for d in range(2):
            o = jnp.concatenate([outs[2 * d], outs[2 * d + 1]], axis=1)
            if second_half:
                o_ref[rows[d], :] = (part_ref[rows[d], :] + o).astype(BF16)
            else:
                part_ref[rows[d], :] = o

    def first_body(t, carry):
        trip(t, False)
        return carry

    def second_body(t, carry):
        trip(t, True)
        return carry

    lax.fori_loop(0, n_chunks // GDN_PREP_CHUNKS, prepare, 0)
    lax.fori_loop(0, n_chunks // 2, first_body, 0)
    lax.fori_loop(n_chunks // 2, n_chunks, second_body, 0)
    if emit_state:
        sout_ref[...] = st_ref[...]


def _gdn_scan(p_main, rows, prm, s0, n_seq, seq_len, row_block0, emit_state, o_prev=None):
    vw = 2 * GDN_DV
    k0 = GDN_KEY_DIM // GDN_DK
    v0 = (2 * GDN_KEY_DIM) // vw
    has_s0 = s0 is not None
    rb = lambda b: row_block0 + b
    st_block = (None, None, 2, 2, GDN_DK, GDN_DV)
    st_map = lambda b, h: (b, 0, 0, h, 0, 0)
    in_specs = [
        pl.BlockSpec((seq_len, GDN_DK), lambda b, h: (rb(b), h)),
        pl.BlockSpec((seq_len, GDN_DK), lambda b, h: (rb(b), k0 + h)),
        pl.BlockSpec((seq_len, vw), lambda b, h: (rb(b), v0 + h)),
        pl.BlockSpec((None, 8, seq_len), lambda b, h: (h, 0, rb(b))),
        pl.BlockSpec((None, 8, LANES), lambda b, h: (h, 0, 0)),
        pl.BlockSpec((None, 8, LANES), lambda b, h: (h, 0, 0)),
        pl.BlockSpec((3, GDN_DK), lambda b, h: (0, h)),
        pl.BlockSpec((3, GDN_DK), lambda b, h: (0, k0 + h)),
        pl.BlockSpec((3, vw), lambda b, h: (0, v0 + h)),
        pl.BlockSpec((1, GDN_DK), lambda b, h: (0, h)),
        pl.BlockSpec((1, GDN_DK), lambda b, h: (0, k0 + h)),
        pl.BlockSpec((1, vw), lambda b, h: (0, v0 + h)),
    ]
    args = [p_main, p_main, p_main, rows, prm["bias"], prm["alog"],
            prm["conv_w"], prm["conv_w"], prm["conv_w"], prm["conv_b"], prm["conv_b"], prm["conv_b"]]
    if has_s0:
        in_specs.append(pl.BlockSpec(st_block, st_map))
        args.append(s0)
    aliases = {}
    if o_prev is not None:
        in_specs.append(pl.BlockSpec(memory_space=pl.ANY))
        args.append(o_prev)
        aliases = {len(args) - 1: 0}
    out_specs = [pl.BlockSpec((seq_len, vw), lambda b, h: (rb(b), h))]
    out_shape = [jax.ShapeDtypeStruct((p_main.shape[0], GDN_VAL_DIM), BF16)]
    if emit_state:
        out_specs.append(pl.BlockSpec(st_block, st_map))
        out_shape.append(jax.ShapeDtypeStruct((n_seq, 1, 2, GDN_V_HEADS, GDN_DK, GDN_DV), F32))
    assert seq_len % (2 * CHUNK) == 0
    outs = pl.pallas_call(
        functools.partial(_gdn_kernel, n_chunks=seq_len // CHUNK, has_s0=has_s0, has_prev=o_prev is not None,
                          emit_state=emit_state),
        grid=(n_seq, GDN_K_HEADS),
        in_specs=in_specs,
        out_specs=out_specs,
        out_shape=out_shape,
        input_output_aliases=aliases,
        scratch_shapes=[
            pltpu.VMEM((4, seq_len, GDN_DV), F32),
            pltpu.VMEM((4, seq_len, GDN_DK), BF16),
            pltpu.VMEM((4, seq_len, GDN_DK + CHUNK), BF16),
            pltpu.VMEM((4, seq_len, CHUNK), BF16),
            pltpu.VMEM((4, seq_len // CHUNK, 1, LANES), F32),
            pltpu.VMEM((seq_len, vw), F32),
            pltpu.VMEM((CHUNK.bit_length() - 2, CHUNK, CHUNK), BF16),
            pltpu.VMEM((2, 2, GDN_DK, GDN_DV), F32),
        ],
        compiler_params=_cparams(("arbitrary", "arbitrary")),
        name="gdn_scan",
    )(*args)
    return outs if emit_state else (outs[0], None)


def _route(logits_t, rbias):
    aff = jax.nn.sigmoid(logits_t)
    sel = aff + rbias
    a = [aff[e:e + 1, :] for e in range(N_EXPERTS)]
    s = [sel[e:e + 1, :] for e in range(N_EXPERTS)]
    gs = []
    for g in range(N_EXPERT_GROUPS):
        v = s[EXPERTS_PER_GROUP * g:EXPERTS_PER_GROUP * (g + 1)]
        best = None
        for x in range(EXPERTS_PER_GROUP):
            for y in range(x + 1, EXPERTS_PER_GROUP):
                ps = v[x] + v[y]
                best = ps if best is None else jnp.maximum(best, ps)
        gs.append(best)
    top, gi = gs[0], jnp.zeros_like(gs[0], dtype=jnp.int32)
    for g in range(1, N_EXPERT_GROUPS):
        up = gs[g] > top
        top = jnp.where(up, gs[g], top)
        gi = jnp.where(up, g, gi)
    ms = [jnp.where(gi == e // EXPERTS_PER_GROUP, s[e], -jnp.inf) for e in range(N_EXPERTS)]
    b1, i1 = ms[0], jnp.zeros_like(gi)
    for e in range(1, N_EXPERTS):
        up = ms[e] > b1
        b1 = jnp.where(up, ms[e], b1)
        i1 = jnp.where(up, e, i1)
    b2, i2 = jnp.full_like(b1, -jnp.inf), jnp.zeros_like(gi)
    for e in range(N_EXPERTS):
        cand = jnp.where(i1 == e, -jnp.inf, ms[e])
        up = cand > b2
        b2 = jnp.where(up, cand, b2)
        i2 = jnp.where(up, e, i2)
    w1 = sum(jnp.where(i1 == e, a[e], 0.0) for e in range(N_EXPERTS))
    w2 = sum(jnp.where(i2 == e, a[e], 0.0) for e in range(N_EXPERTS))
    tot = w1 + w2
    rows = [jnp.where(i1 == e, w1, 0.0) / tot + jnp.where(i2 == e, w2, 0.0) / tot for e in range(N_EXPERTS)]
    return jnp.concatenate(rows, axis=0)


def _post_kernel(y_ref, z_ref, nw_ref, wo_ref, x_ref, mod_ref, n2_ref, wrt_ref, rb_ref,
                 x1_ref, hf_ref, gate_ref, *, group_width, gate_before_norm):
    y = y_ref[...].astype(F32)
    gz = _silu(z_ref[...].astype(F32))
    if gate_before_norm:
        y = y * gz
    parts = [_rms_unit(y[:, s:s + group_width]) for s in range(0, y.shape[1], group_width)]
    yn = jnp.concatenate(parts, axis=1) * nw_ref[...]
    if not gate_before_norm:
        yn = yn * gz
    out = _dot(yn.astype(BF16), wo_ref[...])
    x1 = x_ref[...] + mod_ref[2:3, :] * out
    x1_ref[...] = x1
    hf = _rms_unit(x1) * n2_ref[...] * (1.0 + mod_ref[4:5, :]) + mod_ref[3:4, :]
    hf_ref[...] = hf.astype(BF16)
    logits_t = _dot_nt(wrt_ref[...], hf, precision=HIGHEST)
    gate_ref[...] = _route(logits_t, rb_ref[...])


def _post(y, p_main, z_block0, x, mods, layer, seg_of, norm_w_full, w_out, norm2_w, w_router_t, router_bias,
          group_width, gate_before_norm):
    t, d = x.shape
    inner = y.shape[1]
    tm = seg_of.tile
    return pl.pallas_call(
        functools.partial(_post_kernel, group_width=group_width, gate_before_norm=gate_before_norm),
        grid=(t // tm,),
        in_specs=[
            pl.BlockSpec((tm, inner), lambda i: (i, 0)),
            pl.BlockSpec((tm, inner), lambda i: (i, z_block0)),
            pl.BlockSpec((1, inner), lambda i: (0, 0)),
            pl.BlockSpec((inner, d), lambda i: (0, 0)),
            pl.BlockSpec((tm, d), lambda i: (i, 0)),
            pl.BlockSpec((None, None, 6, d), lambda i: (layer, seg_of(i), 0, 0)),
            pl.BlockSpec((1, d), lambda i: (0, 0)),
            pl.BlockSpec((N_EXPERTS, d), lambda i: (0, 0)),
            pl.BlockSpec((N_EXPERTS, 1), lambda i: (0, 0)),
        ],
        out_specs=[
            pl.BlockSpec((tm, d), lambda i: (i, 0)),
            pl.BlockSpec((tm, d), lambda i: (i, 0)),
            pl.BlockSpec((N_EXPERTS, tm), lambda i: (0, i)),
        ],
        out_shape=[
            jax.ShapeDtypeStruct((t, d), F32),
            jax.ShapeDtypeStruct((t, d), BF16),
            jax.ShapeDtypeStruct((N_EXPERTS, t), F32),
        ],
        compiler_params=_cparams(("arbitrary",)),
        name="post",
    )(y, p_main, norm_w_full.reshape(1, inner), w_out, x, mods, norm2_w.reshape(1, d), w_router_t,
      router_bias.reshape(N_EXPERTS, 1))


def _moe_kernel(hf_ref, gate_ref, tri_ref, wg_ref, wu_ref, wd_ref, x1_ref, mod_ref, fn_ref, o_ref,
                acc_ref, rank_ref, cnt_ref, *, final_norm, n_ctx_tiles):
    e = pl.program_id(1)
    tm = hf_ref.shape[0]

    @pl.when(e == 0)
    def _():
        acc_ref[...] = jnp.zeros(acc_ref.shape, F32)
        sel = jnp.where(gate_ref[...] > 0.0, 1.0, 0.0)
        rank_ref[...] = _dot(sel.astype(BF16), tri_ref[...])
        for k in range(N_EXPERTS):
            cnt_ref[k] = jnp.sum(sel[k:k + 1, :]).astype(jnp.int32)

    g_row = gate_ref[pl.ds(e, 1), :]
    slot = jnp.where(g_row > 0.0, rank_ref[pl.ds(e, 1), :], -1.0)
    sub = lax.broadcasted_iota(jnp.int32, (MOE_ROWS, tm), 0).astype(F32)

    def block(b, carry):
        pf = jnp.where(sub == slot - (b * MOE_ROWS).astype(F32), 1.0, 0.0)
        pb = pf.astype(BF16)
        xg = _dot(pb, hf_ref[...]).astype(BF16)
        hid = _silu(_dot(xg, wg_ref[...])) * _dot(xg, wu_ref[...])
        y = _dot(hid.astype(BF16), wd_ref[...])
        gsel = jnp.sum(pf * g_row, axis=1, keepdims=True)
        acc_ref[...] += _dot_tn(pb, (y * gsel).astype(BF16))
        return carry

    lax.fori_loop(0, (cnt_ref[e] + MOE_ROWS - 1) // MOE_ROWS, block, 0)

    @pl.when(e == pl.num_programs(1) - 1)
    def _():
        x2 = x1_ref[...] + mod_ref[5:6, :] * acc_ref[...]
        if not final_norm:
            o_ref[0][...] = x2
        else:
            y = _rms_unit(x2) * fn_ref[...]
            i = pl.program_id(0)

            @pl.when(i < n_ctx_tiles)
            def _():
                o_ref[0][...] = y

            @pl.when(i >= n_ctx_tiles)
            def _():
                o_ref[1][...] = y


def _moe_kernel_entry(*refs, final_norm, n_ctx_tiles):
    n_out = 2 if final_norm else 1
    _moe_kernel(*refs[:9], refs[9:9 + n_out], *refs[9 + n_out:], final_norm=final_norm, n_ctx_tiles=n_ctx_tiles)


def _moe(hf, gates, wg, wu, wd, x1, mods, layer, seg_of, final_norm_w, final_norm):
    t, d = x1.shape
    n_e, _, de = wg.shape
    tm = seg_of.tile
    nct = seg_of.n_ctx_tiles
    earlier = jnp.triu(jnp.ones((tm, tm), BF16), k=1)
    if final_norm:
        out_specs = [pl.BlockSpec((tm, d), lambda i, e: (jnp.minimum(i, nct - 1), 0)),
                     pl.BlockSpec((tm, d), lambda i, e: (jnp.maximum(i - nct, 0), 0))]
        out_shape = [jax.ShapeDtypeStruct((nct * tm, d), F32), jax.ShapeDtypeStruct((t - nct * tm, d), F32)]
    else:
        out_specs = [pl.BlockSpec((tm, d), lambda i, e: (i, 0))]
        out_shape = [jax.ShapeDtypeStruct((t, d), F32)]
    return pl.pallas_call(
        functools.partial(_moe_kernel_entry, final_norm=final_norm, n_ctx_tiles=nct),
        grid=(t // tm, n_e),
        in_specs=[
            pl.BlockSpec((tm, d), lambda i, e: (i, 0)),
            pl.BlockSpec((N_EXPERTS, tm), lambda i, e: (0, i)),
            pl.BlockSpec((tm, tm), lambda i, e: (0, 0)),
            pl.BlockSpec((None, d, de), lambda i, e: (e, 0, 0)),
            pl.BlockSpec((None, d, de), lambda i, e: (e, 0, 0)),
            pl.BlockSpec((None, de, d), lambda i, e: (e, 0, 0)),
            pl.BlockSpec((tm, d), lambda i, e: (i, 0)),
            pl.BlockSpec((None, None, 6, d), lambda i, e: (layer, seg_of(i), 0, 0)),
            pl.BlockSpec((1, d), lambda i, e: (0, 0)),
        ],
        out_specs=out_specs,
        out_shape=out_shape,
        scratch_shapes=[
            pltpu.VMEM((tm, d), F32),
            pltpu.VMEM((N_EXPERTS, tm), F32),
            pltpu.SMEM((N_EXPERTS,), jnp.int32),
        ],
        compiler_params=_cparams(("arbitrary", "arbitrary")),
        name="moe",
    )(hf, gates, earlier, wg, wu, wd, x1, mods, final_norm_w.reshape(1, d))


def _lanes(v):
    return jnp.broadcast_to(v[..., None].astype(F32), v.shape + (LANES,))


def _ssd_small_order():
    idx = []
    for g in range(SSD_GROUPS):
        for d in range(2):
            for h in range(SSD_HEADS_PER_GROUP):
                idx.append(d * SSD_HEADS + SSD_HEADS_PER_GROUP * g + h)
    return jnp.array(idx, jnp.int32)


def _gdn_small_order():
    idx = []
    for kh in range(GDN_K_HEADS):
        for which in range(2):
            for d in range(2):
                for j in range(2):
                    idx.append(d * 2 * GDN_V_HEADS + which * GDN_V_HEADS + 2 * kh + j)
    return jnp.array(idx, jnp.int32)


def _ssd_params(j, ssd_w_in, ssd_conv_w, ssd_conv_b, ssd_a_log, ssd_dt_bias, ssd_d):
    order = _ssd_small_order()
    w = ssd_w_in[j]
    per_row = lambda v: v.reshape(-1)[order].reshape(SSD_GROUPS, 2 * SSD_HEADS_PER_GROUP)
    return dict(
        w_main=w[:, :SSD_MAIN].astype(BF16),
        w_small_t=w[:, SSD_MAIN:][:, order].T.astype(BF16),
        conv_w=ssd_conv_w[j], conv_b=ssd_conv_b[j].reshape(1, -1),
        bias=_lanes(per_row(ssd_dt_bias[j])), alog=_lanes(per_row(ssd_a_log[j])),
        dskip=jnp.repeat(ssd_d[j].astype(F32), SSD_HEADDIM).reshape(1, SSD_INNER),
    )


def _gdn_params(j, gdn_w_in, gdn_conv_w, gdn_conv_b, gdn_a_log, gdn_dt_bias):
    order = _gdn_small_order()
    w = gdn_w_in[j]
    def per_row(v):
        full = jnp.stack([jnp.zeros_like(v), v], axis=1).astype(F32)
        return full.reshape(-1)[order].reshape(GDN_K_HEADS, 8)
    return dict(
        w_main=w[:, :GDN_MAIN].astype(BF16),
        w_small_t=w[:, GDN_MAIN:][:, order].T.astype(BF16),
        conv_w=gdn_conv_w[j], conv_b=gdn_conv_b[j].reshape(1, -1),
        bias=_lanes(per_row(gdn_dt_bias[j])), alog=_lanes(per_row(gdn_a_log[j])),
    )


def _ssd_state_in(s):
    n = s.shape[0]
    s = s.reshape(n, 2, SSD_GROUPS, SSD_PAIRS_PER_GROUP, 2, SSD_HEADDIM, SSD_STATE)
    return s.transpose(0, 1, 2, 3, 6, 4, 5).reshape(n, 2, SSD_GROUPS, SSD_PAIRS_PER_GROUP, SSD_STATE, LANES)


def _ssd_state_out(s):
    n = s.shape[0]
    s = s.reshape(n, 2, SSD_GROUPS, SSD_PAIRS_PER_GROUP, SSD_STATE, 2, SSD_HEADDIM)
    return s.transpose(0, 1, 2, 3, 5, 6, 4).reshape(n, 2, SSD_HEADS, SSD_HEADDIM, SSD_STATE)


def kernel(x_prompt, x_sample, state_ssd, state_gdn, c, c_ctx, w_ada, b_ada, norm_w, final_norm_w, ssd_w_in, ssd_conv_w, ssd_conv_b, ssd_a_log, ssd_dt_bias, ssd_d, ssd_norm_w, ssd_w_out, gdn_w_in, gdn_conv_w, gdn_conv_b, gdn_a_log, gdn_dt_bias, gdn_norm_w, gdn_w_out, w_router, router_bias, moe_w_gate, moe_w_up, moe_w_down):
    nb, lc, d = x_prompt.shape
    ns, ls, _ = x_sample.shape
    depth = w_ada.shape[0]
    tc, ts = nb * lc, ns * ls
    assert tc % ls == 0 and ns + 1 <= N_MOD_ROWS

    x = _embed(x_prompt, x_sample)
    cond = jnp.concatenate([c_ctx[None, :], c, jnp.zeros((N_MOD_ROWS - 1 - ns, d), F32)], axis=0)
    mods = _ada(cond, w_ada, b_ada).reshape(depth, N_MOD_ROWS, 6, d)

    seg_big = _SegOf(_tile(1024, tc, ls), tc, ls)
    seg_mid = _SegOf(_tile(512, tc, ls), tc, ls)
    w_router_t = w_router.T.astype(F32)
    ssd_states, gdn_states = [], []
    for i in range(depth):
        j = i // 2
        if i % 2 == 0:
            prm = _ssd_params(j, ssd_w_in, ssd_conv_w, ssd_conv_b, ssd_a_log, ssd_dt_bias, ssd_d)
            p_main, p_small_t = _proj(x, mods, i, seg_big, norm_w[i, 0], prm["w_main"], prm["w_small_t"])
            rows = p_small_t.reshape(SSD_GROUPS, 2 * SSD_HEADS_PER_GROUP, tc + ts)
            y, s_ctx = _ssd_scan(p_main, rows, prm, None, nb, lc, 0, True)
            y, _ = _ssd_scan(p_main, rows, prm, _ssd_state_in(state_ssd[:, j].astype(F32)), ns, ls,
                             tc // ls, False, y_prev=y)
            ssd_states.append(_ssd_state_out(s_ctx))
            x, hf, gates = _post(y, p_main, 0, x, mods, i, seg_mid, ssd_norm_w[j], ssd_w_out[j].astype(BF16),
                                 norm_w[i, 1], w_router_t, router_bias, SSD_GROUP_WIDTH, True)
        else:
            prm = _gdn_params(j, gdn_w_in, gdn_conv_w, gdn_conv_b, gdn_a_log, gdn_dt_bias)
            p_main, p_small_t = _proj(x, mods, i, seg_big, norm_w[i, 0], prm["w_main"], prm["w_small_t"])
            rows = p_small_t.reshape(GDN_K_HEADS, 8, tc + ts)
            y, s_ctx = _gdn_scan(p_main, rows, prm, None, nb, lc, 0, True)
            y, _ = _gdn_scan(p_main, rows, prm, state_gdn[:, j:j + 1].astype(F32), ns, ls, tc // ls, False,
                             o_prev=y)
            gdn_states.append(s_ctx[:, 0])
            x, hf, gates = _post(y, p_main, GDN_CONV_DIM // GDN_VAL_DIM, x, mods, i, seg_mid,
                                 jnp.tile(gdn_norm_w[j], GDN_V_HEADS), gdn_w_out[j].astype(BF16),
                                 norm_w[i, 1], w_router_t, router_bias, GDN_DV, False)
        outs = _moe(hf, gates, moe_w_gate[i].astype(BF16), moe_w_up[i].astype(BF16), moe_w_down[i].astype(BF16),
                    x, mods, i, seg_big, final_norm_w, i == depth - 1)
        x = outs[0]
    y_prompt = outs[0].reshape(nb, lc, d)
    y_sample = outs[1].reshape(ns, ls, d)
    return (y_prompt, y_sample, jnp.stack(ssd_states, axis=1), jnp.stack(gdn_states, axis=1))
```

```python
import functools

import jax
import jax.numpy as jnp
from jax import lax
from jax.experimental import pallas as pl
from jax.experimental.pallas import tpu as pltpu

F32 = jnp.float32
BF16 = jnp.bfloat16
HIGHEST = lax.Precision.HIGHEST

D_MODEL = 1024
GRID_W = 64
RMS_EPS = 1e-6

SSD_INNER = 2 * D_MODEL
SSD_HEADDIM = 64
SSD_HEADS = SSD_INNER // SSD_HEADDIM
SSD_GROUPS = 4
SSD_STATE = 128
SSD_HEADS_PER_GROUP = SSD_HEADS // SSD_GROUPS
SSD_PAIRS_PER_GROUP = SSD_HEADS_PER_GROUP // 2
SSD_GROUP_WIDTH = SSD_INNER // SSD_GROUPS
SSD_CONV_DIM = SSD_INNER + 2 * SSD_GROUPS * SSD_STATE
SSD_MAIN = SSD_INNER + SSD_CONV_DIM

GDN_K_HEADS = 8
GDN_V_HEADS = 16
GDN_DK = 128
GDN_DV = 128
GDN_KEY_DIM = GDN_K_HEADS * GDN_DK
GDN_VAL_DIM = GDN_V_HEADS * GDN_DV
GDN_CONV_DIM = 2 * GDN_KEY_DIM + GDN_VAL_DIM
GDN_MAIN = GDN_CONV_DIM + GDN_VAL_DIM

N_EXPERTS = 16
EXPERTS_PER_GROUP = 4
N_EXPERT_GROUPS = N_EXPERTS // EXPERTS_PER_GROUP
D_EXPERT = 512

LANES = 128
CHUNK = 128
GDN_PREP_CHUNKS = 4
MOE_EXPERTS_PER_STEP = 2
assert N_EXPERTS % MOE_EXPERTS_PER_STEP == 0
MOE_ROWS = 160
assert GDN_DK == CHUNK
BF16_ROWS = 16
N_MOD_ROWS = 16
VMEM_LIMIT = 48 * 1024 * 1024
MOE_VMEM_LIMIT = 56 * 1024 * 1024


def _cparams(sem, vmem_limit=VMEM_LIMIT):
    return pltpu.CompilerParams(dimension_semantics=sem, vmem_limit_bytes=vmem_limit)


def _tile(target, *dims):
    t = target
    while any(d % t for d in dims):
        t //= 2
    assert t >= 8
    return t


def _silu(x):
    return x * jax.nn.sigmoid(x)


def _softplus(x):
    return jnp.maximum(x, 0.0) + jnp.log1p(jnp.exp(-jnp.abs(x)))


def _rms_unit(x):
    return x * lax.rsqrt(jnp.mean(x * x, axis=-1, keepdims=True) + RMS_EPS)


def _dot(a, b, **kw):
    return jnp.dot(a, b, preferred_element_type=F32, **kw)


def _dot_nt(a, b, **kw):
    return lax.dot_general(a, b, (((1,), (1,)), ((), ())), preferred_element_type=F32, **kw)


def _dot_tn(a, b):
    return lax.dot_general(a, b, (((0,), (0,)), ((), ())), preferred_element_type=F32)


def _embed_kernel(xp_ref, xs_ref, pos_ref, o_ref, *, n_ctx_tiles):
    i = pl.program_id(0)

    @pl.when(i < n_ctx_tiles)
    def _():
        o_ref[...] = xp_ref[...]

    @pl.when(i >= n_ctx_tiles)
    def _():
        o_ref[...] = xs_ref[...] + pos_ref[...]


def _grid_pos_embed(n_tokens, d):
    t = jnp.arange(n_tokens)
    row = (t // GRID_W).astype(F32)
    col = (t % GRID_W).astype(F32)
    quarter = d // 4
    omega = 1.0 / (10000.0 ** (jnp.arange(quarter, dtype=F32) / quarter))

    def emb(pos):
        ang = pos[:, None] * omega[None, :]
        return jnp.concatenate([jnp.sin(ang), jnp.cos(ang)], axis=-1)

    return jnp.concatenate([emb(row), emb(col)], axis=-1)


def _embed(x_prompt, x_sample):
    nb, lc, d = x_prompt.shape
    ns, ls, _ = x_sample.shape
    tc, ts = nb * lc, ns * ls
    tm = _tile(1024, tc, ls)
    nct, tiles_per_seq = tc // tm, ls // tm
    pos = _grid_pos_embed((ls // GRID_W) * GRID_W, d)
    return pl.pallas_call(
        functools.partial(_embed_kernel, n_ctx_tiles=nct),
        grid=((tc + ts) // tm,),
        in_specs=[
            pl.BlockSpec((tm, d), lambda i: (jnp.minimum(i, nct - 1), 0)),
            pl.BlockSpec((tm, d), lambda i: (jnp.maximum(i - nct, 0), 0)),
            pl.BlockSpec((tm, d), lambda i: (jnp.maximum(i - nct, 0) % tiles_per_seq, 0)),
        ],
        out_specs=pl.BlockSpec((tm, d), lambda i: (i, 0)),
        out_shape=jax.ShapeDtypeStruct((tc + ts, d), F32),
        compiler_params=_cparams(("arbitrary",)),
        name="embed",
    )(x_prompt.reshape(tc, d), x_sample.reshape(ts, d), pos)


def _ada_kernel(c_ref, w_ref, b_ref, o_ref):
    s = _silu(c_ref[...])
    o_ref[...] = _dot(s, w_ref[...], precision=HIGHEST) + b_ref[...]


def _ada(cond, w_ada, b_ada):
    depth, d, six_d = w_ada.shape
    return pl.pallas_call(
        _ada_kernel,
        grid=(depth, six_d // d),
        in_specs=[
            pl.BlockSpec((N_MOD_ROWS, d), lambda l, j: (0, 0)),
            pl.BlockSpec((None, d, d), lambda l, j: (l, 0, j)),
            pl.BlockSpec((None, 1, d), lambda l, j: (l, 0, j)),
        ],
        out_specs=pl.BlockSpec((None, N_MOD_ROWS, d), lambda l, j: (l, 0, j)),
        out_shape=jax.ShapeDtypeStruct((depth, N_MOD_ROWS, six_d), F32),
        compiler_params=_cparams(("arbitrary", "arbitrary")),
        name="ada",
    )(cond, w_ada, b_ada.reshape(depth, 1, six_d))


def _proj_kernel(x_ref, mod_ref, nw_ref, w_ref, wst_ref, o_ref, ost_ref, hm_ref):
    @pl.when(pl.program_id(1) == 0)
    def _():
        hm = _rms_unit(x_ref[...]) * nw_ref[...] * (1.0 + mod_ref[1:2, :]) + mod_ref[0:1, :]
        hmb = hm.astype(BF16)
        hm_ref[...] = hmb
        ost_ref[...] = _dot_nt(wst_ref[...], hmb)

    o_ref[...] = _dot(hm_ref[...], w_ref[...]).astype(BF16)


def _proj(x, mods, layer, seg_of, norm_w, w_main, w_small_t):
    t, d = x.shape
    n_main = w_main.shape[1]
    n_small = w_small_t.shape[0]
    tm = seg_of.tile
    tn = _tile(1024, n_main)
    return pl.pallas_call(
        _proj_kernel,
        grid=(t // tm, n_main // tn),
        in_specs=[
            pl.BlockSpec((tm, d), lambda i, j: (i, 0)),
            pl.BlockSpec((None, None, 6, d), lambda i, j: (layer, seg_of(i), 0, 0)),
            pl.BlockSpec((1, d), lambda i, j: (0, 0)),
            pl.BlockSpec((d, tn), lambda i, j: (0, j)),
            pl.BlockSpec((n_small, d), lambda i, j: (0, 0)),
        ],
        out_specs=[
            pl.BlockSpec((tm, tn), lambda i, j: (i, j)),
            pl.BlockSpec((n_small, tm), lambda i, j: (0, i)),
        ],
        out_shape=[
            jax.ShapeDtypeStruct((t, n_main), BF16),
            jax.ShapeDtypeStruct((n_small, t), F32),
        ],
        scratch_shapes=[pltpu.VMEM((tm, d), BF16)],
        compiler_params=_cparams(("arbitrary", "arbitrary")),
        name="proj",
    )(x, mods, norm_w.reshape(1, d), w_main, w_small_t)


class _SegOf:
    def __init__(self, tile, n_ctx_tokens, sample_len):
        assert n_ctx_tokens % tile == 0 and sample_len % tile == 0
        self.tile = tile
        self.n_ctx_tiles = n_ctx_tokens // tile
        self.tiles_per_sample = sample_len // tile

    def __call__(self, i):
        return jnp.where(i < self.n_ctx_tiles, 0, 1 + (i - self.n_ctx_tiles) // self.tiles_per_sample)


def _chunk_consts():
    ii = lax.broadcasted_iota(jnp.int32, (CHUNK, CHUNK), 0)
    jj = lax.broadcasted_iota(jnp.int32, (CHUNK, CHUNK), 1)
    return ii, jj


def _conv_silu(src_ref, c, n_chunks, w_ref, b_ref):
    seq_len = n_chunks * CHUNK
    r0 = pl.multiple_of(c * CHUNK, CHUNK)
    xb = src_ref[pl.ds(r0, CHUNK), :].astype(F32)
    lo = pl.multiple_of(jnp.maximum(r0 - BF16_ROWS, 0), BF16_ROWS)
    hi = pl.multiple_of(jnp.minimum(r0 + CHUNK, seq_len - BF16_ROWS), BF16_ROWS)
    prev_row = src_ref[pl.ds(lo, BF16_ROWS), :].astype(F32)[BF16_ROWS - 1:BF16_ROWS, :]
    next_row = src_ref[pl.ds(hi, BF16_ROWS), :].astype(F32)[0:1, :]
    prev_row = prev_row * jnp.where(c > 0, 1.0, 0.0)
    next_row = next_row * jnp.where(c < n_chunks - 1, 1.0, 0.0)
    row = lax.broadcasted_iota(jnp.int32, xb.shape, 0)
    xp = jnp.where(row == 0, prev_row, pltpu.roll(xb, 1, 0))
    xn = jnp.where(row == CHUNK - 1, next_row, pltpu.roll(xb, CHUNK - 1, 0))
    y = w_ref[0:1, :] * xp + w_ref[1:2, :] * xb + w_ref[2:3, :] * xn + b_ref[...]
    return _silu(y)


def _rows_to_cols(*row_blocks):
    n = sum(b.shape[0] for b in row_blocks)
    pad = jnp.zeros((LANES - n, CHUNK), F32)
    return jnp.concatenate(list(row_blocks) + [pad], axis=0).T


def _ssd_kernel(*refs, n_chunks, has_s0, has_prev, emit_state):
    it = iter(refs)
    x_ref, b_ref, c_ref, rows_ref, bias_ref, alog_ref = (next(it) for _ in range(6))
    cwx_ref, cwb_ref, cwc_ref, cbx_ref, cbb_ref, cbc_ref, dsk_ref = (next(it) for _ in range(7))
    s0_ref = next(it) if has_s0 else None
    if has_prev:
        next(it)
    y_ref = next(it)
    sout_ref = next(it) if emit_state else None
    xc_ref, cc_ref, bt_ref, g_ref, cols_ref, arow_ref, lrow_ref, part_ref, st_ref = (next(it) for _ in range(9))

    ii, jj = _chunk_consts()
    lane = lax.broadcasted_iota(jnp.int32, (CHUNK, LANES), 1)
    first_head = lane < SSD_HEADDIM
    nh = SSD_HEADS_PER_GROUP
    spread = (lax.broadcasted_iota(jnp.int32, (LANES, SSD_GROUP_WIDTH), 1) // SSD_HEADDIM
              == lax.broadcasted_iota(jnp.int32, (LANES, SSD_GROUP_WIDTH), 0)).astype(BF16)
    keep = ((ii >= jj), (ii <= jj))
    cum = ((ii <= jj).astype(F32), (ii >= jj).astype(F32))
    edge = (CHUNK - 1, 0)

    if has_s0:
        st_ref[...] = s0_ref[...]
    else:
        st_ref[...] = jnp.zeros(st_ref.shape, F32)

    def prepare(c, carry):
        rows = pl.ds(pl.multiple_of(c * CHUNK, CHUNK), CHUNK)
        xc_ref[rows, :] = _conv_silu(x_ref, c, n_chunks, cwx_ref, cbx_ref).astype(BF16)
        bc = _conv_silu(b_ref, c, n_chunks, cwb_ref, cbb_ref)
        cb = _conv_silu(c_ref, c, n_chunks, cwc_ref, cbc_ref).astype(BF16)
        cc_ref[rows, :] = cb
        bt_ref[rows, :] = bc.T.astype(BF16)
        g_ref[rows, :] = _dot_nt(cb, bc.astype(BF16))
        for d in range(2):
            sl = slice(nh * d, nh * d + nh)
            dt = _softplus(rows_ref[sl, rows] + bias_ref[sl, :])
            acs = _dot(dt * (-jnp.exp(alog_ref[sl, :])), cum[d], precision=HIGHEST)
            arow_ref[d, :, rows] = acs
            lrow_ref[d, :, rows] = jnp.log(dt) - acs
            cols_ref[d, rows, :] = _rows_to_cols(acs, dt)
        return carry

    def trip(t, second_half):
        cidx = (t, n_chunks - 1 - t)
        rows = [pl.ds(pl.multiple_of(c * CHUNK, CHUNK), CHUNK) for c in cidx]
        chains = [(d, p) for d in range(2) for p in range(SSD_PAIRS_PER_GROUP)]
        st = {ch: st_ref[ch[0], ch[1]] for ch in chains}
        xcb = [xc_ref[r, :] for r in rows]
        cb = [cc_ref[r, :] for r in rows]
        bt = [bt_ref[r, :] for r in rows]
        g = [g_ref[r, :] for r in rows]
        cols = [cols_ref[d, rows[d], :] for d in range(2)]
        arow = [arow_ref[d, :, rows[d]] for d in range(2)]
        lrow = [lrow_ref[d, :, rows[d]] for d in range(2)]
        zero = jnp.zeros((CHUNK, LANES), BF16)
        intra, inter = {}, {}
        for d, p in chains:
            xpb = xcb[d][:, LANES * p:LANES * (p + 1)]
            ms = []
            for k in range(2):
                h = 2 * p + k
                decay = jnp.exp(jnp.where(keep[d], cols[d][:, h:h + 1] + lrow[d][h:h + 1, :], -jnp.inf))
                ms.append((g[d] * decay).astype(BF16))
            xblk = jnp.concatenate([jnp.where(first_head, xpb, zero), jnp.where(first_head, zero, xpb)], axis=0)
            intra[(d, p)] = _dot(jnp.concatenate(ms, axis=1), xblk)
            inter[(d, p)] = _dot(cb[d], st[(d, p)].astype(BF16))
        ea, wx = [], []
        for d in range(2):
            last_row = cols[d][edge[d]:edge[d] + 1, :]
            dt_cols = pltpu.roll(cols[d], LANES - nh, 1)
            ea.append(_dot(jnp.exp(cols[d]).astype(BF16), spread))
            wx.append(_dot((jnp.exp(last_row - cols[d]) * dt_cols).astype(BF16), spread))
        ys, new = {}, {}
        for d, p in chains:
            h0, h1 = 2 * p, 2 * p + 1
            pair = slice(LANES * p, LANES * (p + 1))
            last = jnp.where(first_head[0:1, :], arow[d][h0:h0 + 1, edge[d]:edge[d] + 1],
                             arow[d][h1:h1 + 1, edge[d]:edge[d] + 1])
            ys[(d, p)] = intra[(d, p)] + inter[(d, p)] * ea[d][:, pair]
            xt = (xcb[d][:, pair].astype(F32) * wx[d][:, pair]).astype(BF16)
            new[(d, p)] = st[(d, p)] * jnp.exp(last) + _dot(bt[d], xt)
        for d, p in chains:
            st_ref[d, p] = new[(d, p)]
        for d in range(2):
            y = jnp.concatenate([ys[(d, p)] for p in range(SSD_PAIRS_PER_GROUP)], axis=1)
            if second_half:
                skip = dsk_ref[...] * xcb[d].astype(F32)
                y_ref[rows[d], :] = (part_ref[rows[d], :] + y + skip).astype(BF16)
            else:
                part_ref[rows[d], :] = y

    def first_body(t, carry):
        trip(t, False)
        return carry

    def second_body(t, carry):
        trip(t, True)
        return carry

    lax.fori_loop(0, n_chunks, prepare, 0)
    lax.fori_loop(0, n_chunks // 2, first_body, 0)
    lax.fori_loop(n_chunks // 2, n_chunks, second_body, 0)
    if emit_state:
        sout_ref[...] = st_ref[...]


def _ssd_scan(p_main, rows, prm, s0, n_seq, seq_len, row_block0, emit_state, y_prev=None):
    t = p_main.shape[0]
    gw = SSD_GROUP_WIDTH
    x0 = SSD_INNER // gw
    b0 = (2 * SSD_INNER) // SSD_STATE
    c0 = b0 + SSD_GROUPS
    cx0 = 0
    cb0 = SSD_INNER // SSD_STATE
    cc0 = cb0 + SSD_GROUPS
    has_s0 = s0 is not None
    rb = lambda b: row_block0 + b
    st_block = (None, 2, None, SSD_PAIRS_PER_GROUP, SSD_STATE, LANES)
    st_map = lambda b, g: (b, 0, g, 0, 0, 0)
    in_specs = [
        pl.BlockSpec((seq_len, gw), lambda b, g: (rb(b), x0 + g)),
        pl.BlockSpec((seq_len, SSD_STATE), lambda b, g: (rb(b), b0 + g)),
        pl.BlockSpec((seq_len, SSD_STATE), lambda b, g: (rb(b), c0 + g)),
        pl.BlockSpec((None, 2 * SSD_HEADS_PER_GROUP, seq_len), lambda b, g: (g, 0, rb(b))),
        pl.BlockSpec((None, 2 * SSD_HEADS_PER_GROUP, LANES), lambda b, g: (g, 0, 0)),
        pl.BlockSpec((None, 2 * SSD_HEADS_PER_GROUP, LANES), lambda b, g: (g, 0, 0)),
        pl.BlockSpec((3, gw), lambda b, g: (0, cx0 + g)),
        pl.BlockSpec((3, SSD_STATE), lambda b, g: (0, cb0 + g)),
        pl.BlockSpec((3, SSD_STATE), lambda b, g: (0, cc0 + g)),
        pl.BlockSpec((1, gw), lambda b, g: (0, cx0 + g)),
        pl.BlockSpec((1, SSD_STATE), lambda b, g: (0, cb0 + g)),
        pl.BlockSpec((1, SSD_STATE), lambda b, g: (0, cc0 + g)),
        pl.BlockSpec((1, gw), lambda b, g: (0, g)),
    ]
    args = [p_main, p_main, p_main, rows, prm["bias"], prm["alog"],
            prm["conv_w"], prm["conv_w"], prm["conv_w"], prm["conv_b"], prm["conv_b"], prm["conv_b"], prm["dskip"]]
    if has_s0:
        in_specs.append(pl.BlockSpec(st_block, st_map))
        args.append(s0)
    aliases = {}
    if y_prev is not None:
        in_specs.append(pl.BlockSpec(memory_space=pl.ANY))
        args.append(y_prev)
        aliases = {len(args) - 1: 0}
    out_specs = [pl.BlockSpec((seq_len, gw), lambda b, g: (rb(b), g))]
    out_shape = [jax.ShapeDtypeStruct((t, SSD_INNER), BF16)]
    if emit_state:
        out_specs.append(pl.BlockSpec(st_block, st_map))
        out_shape.append(jax.ShapeDtypeStruct(
            (n_seq, 2, SSD_GROUPS, SSD_PAIRS_PER_GROUP, SSD_STATE, LANES), F32))
    assert t % seq_len == 0 and seq_len % (2 * CHUNK) == 0 and SSD_STATE == CHUNK
    outs = pl.pallas_call(
        functools.partial(_ssd_kernel, n_chunks=seq_len // CHUNK, has_s0=has_s0, has_prev=y_prev is not None,
                          emit_state=emit_state),
        grid=(n_seq, SSD_GROUPS),
        in_specs=in_specs,
        out_specs=out_specs,
        out_shape=out_shape,
        input_output_aliases=aliases,
        scratch_shapes=[
            pltpu.VMEM((seq_len, gw), BF16),
            pltpu.VMEM((seq_len, SSD_STATE), BF16),
            pltpu.VMEM((seq_len, CHUNK), BF16),
            pltpu.VMEM((seq_len, CHUNK), F32),
            pltpu.VMEM((2, seq_len, LANES), F32),
            pltpu.VMEM((2, SSD_HEADS_PER_GROUP, seq_len), F32),
            pltpu.VMEM((2, SSD_HEADS_PER_GROUP, seq_len), F32),
            pltpu.VMEM((seq_len, gw), F32),
            pltpu.VMEM((2, SSD_PAIRS_PER_GROUP, SSD_STATE, LANES), F32),
        ],
        compiler_params=_cparams(("arbitrary", "arbitrary")),
        name="ssd_scan",
    )(*args)
    return outs if emit_state else (outs[0], None)


def _merge_masks(ii, jj):
    masks = []
    s = 1
    while s < CHUNK:
        masks.append(((ii // s) != (jj // s)) & ((ii // (2 * s)) == (jj // (2 * s))))
        s *= 2
    return masks


def _gdn_kernel(*refs, n_chunks, has_s0, has_prev, emit_state):
    it = iter(refs)
    q_ref, k_ref, v_ref, rows_ref, bias_ref, alog_ref = (next(it) for _ in range(6))
    cwq_ref, cwk_ref, cwv_ref, cbq_ref, cbk_ref, cbv_ref = (next(it) for _ in range(6))
    s0_ref = next(it) if has_s0 else None
    if has_prev:
        next(it)
    o_ref = next(it)
    sout_ref = next(it) if emit_state else None
    u0_ref, w_ref, qq_ref, kt_ref, el_ref, part_ref, mk_ref, rk_ref, st_ref = (next(it) for _ in range(9))

    ii, jj = _chunk_consts()
    eye = (ii == jj).astype(F32)
    masks = _merge_masks(ii, jj)
    for lvl, mask in enumerate(masks[1:]):
        mk_ref[lvl] = mask.astype(BF16)
    keep = ((ii >= jj), (ii <= jj))
    cum = ((ii <= jj).astype(F32), (ii >= jj).astype(F32))
    edge = (CHUNK - 1, 0)
    chains = [(d, j) for d in range(2) for j in range(2)]

    if has_s0:
        st_ref[...] = s0_ref[...]
    else:
        st_ref[...] = jnp.zeros(st_ref.shape, F32)

    prep = min(GDN_PREP_CHUNKS, n_chunks)
    assert n_chunks % prep == 0

    def setup(p):
        work = []
        for cc in range(prep):
            c = p * prep + cc
            rows = pl.ds(pl.multiple_of(c * CHUNK, CHUNK), CHUNK)
            qc = _conv_silu(q_ref, c, n_chunks, cwq_ref, cbq_ref)
            kc = _conv_silu(k_ref, c, n_chunks, cwk_ref, cbk_ref)
            vc = _conv_silu(v_ref, c, n_chunks, cwv_ref, cbv_ref)
            qn = qc * (lax.rsqrt(jnp.sum(qc * qc, axis=-1, keepdims=True) + RMS_EPS) * (GDN_DK ** -0.5))
            kn = kc * lax.rsqrt(jnp.sum(kc * kc, axis=-1, keepdims=True) + RMS_EPS)
            knt = kn.T
            kntb = knt.astype(BF16)
            kk = _dot(kn.astype(BF16), kntb)
            qk = _dot(qn.astype(BF16), kntb)
            raw = rows_ref[:, rows]
            beta = jax.nn.sigmoid(raw)
            gl = -jnp.exp(alog_ref[...]) * _softplus(raw + bias_ref[...])
            gcum = [_dot(gl, cum[d], precision=HIGHEST) for d in range(2)]
            cols = _rows_to_cols(beta, gcum[0], gcum[1])
            for d in range(2):
                for j in range(2):
                    rg = 4 + 2 * d + j
                    g_r = gcum[d][rg:rg + 1, :]
                    b_f = jnp.broadcast_to(cols[:, 2 * d + j:2 * d + j + 1], (CHUNK, LANES))
                    g_f = jnp.broadcast_to(cols[:, 8 + 8 * d + rg:8 + 8 * d + rg + 1], (CHUNK, LANES))
                    dec = jnp.exp(jnp.where(keep[d], g_f - g_r, -jnp.inf))
                    m = kk * dec * b_f
                    eg = jnp.exp(g_f)
                    last = gcum[d][rg:rg + 1, edge[d]:edge[d] + 1]
                    ch = 2 * d + j
                    qq_ref[ch, rows, :] = jnp.concatenate([qn * eg, qk * dec], axis=1).astype(BF16)
                    kt_ref[ch, rows, :] = (knt * jnp.exp(last - g_r)).astype(BF16)
                    el_ref[ch, c] = jnp.broadcast_to(jnp.exp(last), (1, LANES))
                    u0_ref[ch, rows, :] = vc[:, GDN_DV * j:GDN_DV * (j + 1)] * b_f
                    rk_ref[len(work)] = kn * (b_f * eg)
                    work.append(dict(mb=m.astype(BF16), tinv=eye - jnp.where(masks[0], m, 0.0)))
        return work

    def finish(p, work):
        for lvl in range(len(masks) - 1):
            for wk in work:
                tb = wk["tinv"].astype(BF16)
                y = _dot(tb, wk["mb"] * mk_ref[lvl])
                wk["tinv"] = wk["tinv"] - _dot(y.astype(BF16), tb)
        for i, wk in enumerate(work):
            c = p * prep + i // 4
            rows = pl.ds(pl.multiple_of(c * CHUNK, CHUNK), CHUNK)
            rhs = jnp.concatenate([u0_ref[i % 4, rows, :], rk_ref[i]], axis=1)
            sol = rhs + _dot((wk["tinv"] - eye).astype(BF16), rhs.astype(BF16))
            u0_ref[i % 4, rows, :] = sol[:, :GDN_DV]
            w_ref[i % 4, rows, :] = sol[:, GDN_DV:].astype(BF16)

    def prepare(p, carry):
        finish(p, setup(p))
        return carry

    def trip(t, second_half):
        cidx = (t, n_chunks - 1 - t)
        rows = [pl.ds(pl.multiple_of(c * CHUNK, CHUNK), CHUNK) for c in cidx]
        st = [st_ref[d, j] for d, j in chains]
        sb = [s.astype(BF16) for s in st]
        ws = [_dot(w_ref[2 * d + j, rows[d], :], sb[2 * d + j]) for d, j in chains]
        ub = [(u0_ref[2 * d + j, rows[d], :] - ws[2 * d + j]).astype(BF16) for d, j in chains]
        outs = [_dot(qq_ref[2 * d + j, rows[d], :], jnp.concatenate([sb[2 * d + j], ub[2 * d + j]], axis=0))
                for d, j in chains]
        new = [st[2 * d + j] * el_ref[2 * d + j, cidx[d]] + _dot(kt_ref[2 * d + j, rows[d], :], ub[2 * d + j])
               for d, j in chains]
        for d, j in chains:
            st_ref[d, j] = new[2 * d + j]
        for d in range(2):
            o = jnp.concatenate([outs[2 * d], outs[2 * d + 1]], axis=1)
            if second_half:
                o_ref[rows[d], :] = (part_ref[rows[d], :] + o).astype(BF16)
            else:
                part_ref[rows[d], :] = o

    def first_body(t, carry):
        trip(t, False)
        return carry

    def second_body(t, carry):
        trip(t, True)
        return carry

    lax.fori_loop(0, n_chunks // prep, prepare, 0)
    lax.fori_loop(0, n_chunks // 2, first_body, 0)
    lax.fori_loop(n_chunks // 2, n_chunks, second_body, 0)
    if emit_state:
        sout_ref[...] = st_ref[...]


def _gdn_scan(p_main, rows, prm, s0, n_seq, seq_len, row_block0, emit_state, o_prev=None):
    vw = 2 * GDN_DV
    k0 = GDN_KEY_DIM // GDN_DK
    v0 = (2 * GDN_KEY_DIM) // vw
    has_s0 = s0 is not None
    rb = lambda b: row_block0 + b
    st_block = (None, None, 2, 2, GDN_DK, GDN_DV)
    st_map = lambda b, h: (b, 0, 0, h, 0, 0)
    in_specs = [
        pl.BlockSpec((seq_len, GDN_DK), lambda b, h: (rb(b), h)),
        pl.BlockSpec((seq_len, GDN_DK), lambda b, h: (rb(b), k0 + h)),
        pl.BlockSpec((seq_len, vw), lambda b, h: (rb(b), v0 + h)),
        pl.BlockSpec((None, 8, seq_len), lambda b, h: (h, 0, rb(b))),
        pl.BlockSpec((None, 8, LANES), lambda b, h: (h, 0, 0)),
        pl.BlockSpec((None, 8, LANES), lambda b, h: (h, 0, 0)),
        pl.BlockSpec((3, GDN_DK), lambda b, h: (0, h)),
        pl.BlockSpec((3, GDN_DK), lambda b, h: (0, k0 + h)),
        pl.BlockSpec((3, vw), lambda b, h: (0, v0 + h)),
        pl.BlockSpec((1, GDN_DK), lambda b, h: (0, h)),
        pl.BlockSpec((1, GDN_DK), lambda b, h: (0, k0 + h)),
        pl.BlockSpec((1, vw), lambda b, h: (0, v0 + h)),
    ]
    args = [p_main, p_main, p_main, rows, prm["bias"], prm["alog"],
            prm["conv_w"], prm["conv_w"], prm["conv_w"], prm["conv_b"], prm["conv_b"], prm["conv_b"]]
    if has_s0:
        in_specs.append(pl.BlockSpec(st_block, st_map))
        args.append(s0)
    aliases = {}
    if o_prev is not None:
        in_specs.append(pl.BlockSpec(memory_space=pl.ANY))
        args.append(o_prev)
        aliases = {len(args) - 1: 0}
    out_specs = [pl.BlockSpec((seq_len, vw), lambda b, h: (rb(b), h))]
    out_shape = [jax.ShapeDtypeStruct((p_main.shape[0], GDN_VAL_DIM), BF16)]
    if emit_state:
        out_specs.append(pl.BlockSpec(st_block, st_map))
        out_shape.append(jax.ShapeDtypeStruct((n_seq, 1, 2, GDN_V_HEADS, GDN_DK, GDN_DV), F32))
    assert seq_len % (2 * CHUNK) == 0
    outs = pl.pallas_call(
        functools.partial(_gdn_kernel, n_chunks=seq_len // CHUNK, has_s0=has_s0, has_prev=o_prev is not None,
                          emit_state=emit_state),
        grid=(n_seq, GDN_K_HEADS),
        in_specs=in_specs,
        out_specs=out_specs,
        out_shape=out_shape,
        input_output_aliases=aliases,
        scratch_shapes=[
            pltpu.VMEM((4, seq_len, GDN_DV), F32),
            pltpu.VMEM((4, seq_len, GDN_DK), BF16),
            pltpu.VMEM((4, seq_len, GDN_DK + CHUNK), BF16),
            pltpu.VMEM((4, seq_len, CHUNK), BF16),
            pltpu.VMEM((4, seq_len // CHUNK, 1, LANES), F32),
            pltpu.VMEM((seq_len, vw), F32),
            pltpu.VMEM((CHUNK.bit_length() - 2, CHUNK, CHUNK), BF16),
            pltpu.VMEM((4 * GDN_PREP_CHUNKS, CHUNK, GDN_DK), F32),
            pltpu.VMEM((2, 2, GDN_DK, GDN_DV), F32),
        ],
        compiler_params=_cparams(("arbitrary", "arbitrary")),
        name="gdn_scan",
    )(*args)
    return outs if emit_state else (outs[0], None)


def _route(logits_t, rbias):
    aff = jax.nn.sigmoid(logits_t)
    sel = aff + rbias
    a = [aff[e:e + 1, :] for e in range(N_EXPERTS)]
    s = [sel[e:e + 1, :] for e in range(N_EXPERTS)]
    gs = []
    for g in range(N_EXPERT_GROUPS):
        v = s[EXPERTS_PER_GROUP * g:EXPERTS_PER_GROUP * (g + 1)]
        best = None
        for x in range(EXPERTS_PER_GROUP):
            for y in range(x + 1, EXPERTS_PER_GROUP):
                ps = v[x] + v[y]
                best = ps if best is None else jnp.maximum(best, ps)
        gs.append(best)
    top, gi = gs[0], jnp.zeros_like(gs[0], dtype=jnp.int32)
    for g in range(1, N_EXPERT_GROUPS):
        up = gs[g] > top
        top = jnp.where(up, gs[g], top)
        gi = jnp.where(up, g, gi)
    ms = [jnp.where(gi == e // EXPERTS_PER_GROUP, s[e], -jnp.inf) for e in range(N_EXPERTS)]
    b1, i1 = ms[0], jnp.zeros_like(gi)
    for e in range(1, N_EXPERTS):
        up = ms[e] > b1
        b1 = jnp.where(up, ms[e], b1)
        i1 = jnp.where(up, e, i1)
    b2, i2 = jnp.full_like(b1, -jnp.inf), jnp.zeros_like(gi)
    for e in range(N_EXPERTS):
        cand = jnp.where(i1 == e, -jnp.inf, ms[e])
        up = cand > b2
        b2 = jnp.where(up, cand, b2)
        i2 = jnp.where(up, e, i2)
    w1 = sum(jnp.where(i1 == e, a[e], 0.0) for e in range(N_EXPERTS))
    w2 = sum(jnp.where(i2 == e, a[e], 0.0) for e in range(N_EXPERTS))
    tot = w1 + w2
    rows = [jnp.where(i1 == e, w1, 0.0) / tot + jnp.where(i2 == e, w2, 0.0) / tot for e in range(N_EXPERTS)]
    return jnp.concatenate(rows, axis=0)


def _post_kernel(y_ref, z_ref, nw_ref, wo_ref, x_ref, mod_ref, n2_ref, wrt_ref, rb_ref,
                 x1_ref, hf_ref, gate_ref, *, group_width, gate_before_norm):
    y = y_ref[...].astype(F32)
    gz = _silu(z_ref[...].astype(F32))
    if gate_before_norm:
        y = y * gz
    parts = [_rms_unit(y[:, s:s + group_width]) for s in range(0, y.shape[1], group_width)]
    yn = jnp.concatenate(parts, axis=1) * nw_ref[...]
    if not gate_before_norm:
        yn = yn * gz
    out = _dot(yn.astype(BF16), wo_ref[...])
    x1 = x_ref[...] + mod_ref[2:3, :] * out
    x1_ref[...] = x1
    hf = _rms_unit(x1) * n2_ref[...] * (1.0 + mod_ref[4:5, :]) + mod_ref[3:4, :]
    hf_ref[...] = hf.astype(BF16)
    logits_t = _dot_nt(wrt_ref[...], hf, precision=HIGHEST)
    gate_ref[...] = _route(logits_t, rb_ref[...])


def _post(y, p_main, z_block0, x, mods, layer, seg_of, norm_w_full, w_out, norm2_w, w_router_t, router_bias,
          group_width, gate_before_norm):
    t, d = x.shape
    inner = y.shape[1]
    tm = seg_of.tile
    return pl.pallas_call(
        functools.partial(_post_kernel, group_width=group_width, gate_before_norm=gate_before_norm),
        grid=(t // tm,),
        in_specs=[
            pl.BlockSpec((tm, inner), lambda i: (i, 0)),
            pl.BlockSpec((tm, inner), lambda i: (i, z_block0)),
            pl.BlockSpec((1, inner), lambda i: (0, 0)),
            pl.BlockSpec((inner, d), lambda i: (0, 0)),
            pl.BlockSpec((tm, d), lambda i: (i, 0)),
            pl.BlockSpec((None, None, 6, d), lambda i: (layer, seg_of(i), 0, 0)),
            pl.BlockSpec((1, d), lambda i: (0, 0)),
            pl.BlockSpec((N_EXPERTS, d), lambda i: (0, 0)),
            pl.BlockSpec((N_EXPERTS, 1), lambda i: (0, 0)),
        ],
        out_specs=[
            pl.BlockSpec((tm, d), lambda i: (i, 0)),
            pl.BlockSpec((tm, d), lambda i: (i, 0)),
            pl.BlockSpec((N_EXPERTS, tm), lambda i: (0, i)),
        ],
        out_shape=[
            jax.ShapeDtypeStruct((t, d), F32),
            jax.ShapeDtypeStruct((t, d), BF16),
            jax.ShapeDtypeStruct((N_EXPERTS, t), F32),
        ],
        compiler_params=_cparams(("arbitrary",)),
        name="post",
    )(y, p_main, norm_w_full.reshape(1, inner), w_out, x, mods, norm2_w.reshape(1, d), w_router_t,
      router_bias.reshape(N_EXPERTS, 1))


def _moe_kernel(hf_ref, gate_ref, tri_ref, wg_ref, wu_ref, wd_ref, x1_ref, mod_ref, fn_ref, o_ref,
                acc_ref, rank_ref, cnt_ref, *, final_norm, n_ctx_tiles):
    e = pl.program_id(1)
    tm = hf_ref.shape[0]

    @pl.when(e == 0)
    def _():
        acc_ref[...] = jnp.zeros(acc_ref.shape, F32)
        sel = jnp.where(gate_ref[...] > 0.0, 1.0, 0.0)
        rank_ref[...] = _dot(sel.astype(BF16), tri_ref[...])
        for k in range(N_EXPERTS):
            cnt_ref[k] = jnp.sum(sel[k:k + 1, :]).astype(jnp.int32)

    sub = lax.broadcasted_iota(jnp.int32, (MOE_ROWS, tm), 0).astype(F32)
    for k in range(MOE_EXPERTS_PER_STEP):
        ex = e * MOE_EXPERTS_PER_STEP + k
        g_row = gate_ref[pl.ds(ex, 1), :]
        slot = jnp.where(g_row > 0.0, rank_ref[pl.ds(ex, 1), :], -1.0)

        def block(b, carry, k=k, g_row=g_row, slot=slot):
            pf = jnp.where(sub == slot - (b * MOE_ROWS).astype(F32), 1.0, 0.0)
            pb = pf.astype(BF16)
            xg = _dot(pb, hf_ref[...]).astype(BF16)
            hid = _silu(_dot(xg, wg_ref[k])) * _dot(xg, wu_ref[k])
            y = _dot(hid.astype(BF16), wd_ref[k])
            gsel = jnp.sum(pf * g_row, axis=1, keepdims=True)
            acc_ref[...] += _dot_tn(pb, (y * gsel).astype(BF16))
            return carry

        lax.fori_loop(0, (cnt_ref[ex] + MOE_ROWS - 1) // MOE_ROWS, block, 0)

    @pl.when(e == pl.num_programs(1) - 1)
    def _():
        x2 = x1_ref[...] + mod_ref[5:6, :] * acc_ref[...]
        if not final_norm:
            o_ref[0][...] = x2
        else:
            y = _rms_unit(x2) * fn_ref[...]
            i = pl.program_id(0)

            @pl.when(i < n_ctx_tiles)
            def _():
                o_ref[0][...] = y

            @pl.when(i >= n_ctx_tiles)
            def _():
                o_ref[1][...] = y


def _moe_kernel_entry(*refs, final_norm, n_ctx_tiles):
    n_out = 2 if final_norm else 1
    _moe_kernel(*refs[:9], refs[9:9 + n_out], *refs[9 + n_out:], final_norm=final_norm, n_ctx_tiles=n_ctx_tiles)


def _moe(hf, gates, wg, wu, wd, x1, mods, layer, seg_of, final_norm_w, final_norm):
    t, d = x1.shape
    n_e, _, de = wg.shape
    tm = seg_of.tile
    nct = seg_of.n_ctx_tiles
    earlier = jnp.triu(jnp.ones((tm, tm), BF16), k=1)
    if final_norm:
        out_specs = [pl.BlockSpec((tm, d), lambda i, e: (jnp.minimum(i, nct - 1), 0)),
                     pl.BlockSpec((tm, d), lambda i, e: (jnp.maximum(i - nct, 0), 0))]
        out_shape = [jax.ShapeDtypeStruct((nct * tm, d), F32), jax.ShapeDtypeStruct((t - nct * tm, d), F32)]
    else:
        out_specs = [pl.BlockSpec((tm, d), lambda i, e: (i, 0))]
        out_shape = [jax.ShapeDtypeStruct((t, d), F32)]
    return pl.pallas_call(
        functools.partial(_moe_kernel_entry, final_norm=final_norm, n_ctx_tiles=nct),
        grid=(t // tm, n_e // MOE_EXPERTS_PER_STEP),
        in_specs=[
            pl.BlockSpec((tm, d), lambda i, e: (i, 0)),
            pl.BlockSpec((N_EXPERTS, tm), lambda i, e: (0, i)),
            pl.BlockSpec((tm, tm), lambda i, e: (0, 0)),
            pl.BlockSpec((MOE_EXPERTS_PER_STEP, d, de), lambda i, e: (e, 0, 0)),
            pl.BlockSpec((MOE_EXPERTS_PER_STEP, d, de), lambda i, e: (e, 0, 0)),
            pl.BlockSpec((MOE_EXPERTS_PER_STEP, de, d), lambda i, e: (e, 0, 0)),
            pl.BlockSpec((tm, d), lambda i, e: (i, 0)),
            pl.BlockSpec((None, None, 6, d), lambda i, e: (layer, seg_of(i), 0, 0)),
            pl.BlockSpec((1, d), lambda i, e: (0, 0)),
        ],
        out_specs=out_specs,
        out_shape=out_shape,
        scratch_shapes=[
            pltpu.VMEM((tm, d), F32),
            pltpu.VMEM((N_EXPERTS, tm), F32),
            pltpu.SMEM((N_EXPERTS,), jnp.int32),
        ],
        compiler_params=_cparams(("arbitrary", "arbitrary"), MOE_VMEM_LIMIT),
        name="moe",
    )(hf, gates, earlier, wg, wu, wd, x1, mods, final_norm_w.reshape(1, d))


def _lanes(v):
    return jnp.broadcast_to(v[..., None].astype(F32), v.shape + (LANES,))


def _ssd_small_order():
    idx = []
    for g in range(SSD_GROUPS):
        for d in range(2):
            for h in range(SSD_HEADS_PER_GROUP):
                idx.append(d * SSD_HEADS + SSD_HEADS_PER_GROUP * g + h)
    return jnp.array(idx, jnp.int32)


def _gdn_small_order():
    idx = []
    for kh in range(GDN_K_HEADS):
        for which in range(2):
            for d in range(2):
                for j in range(2):
                    idx.append(d * 2 * GDN_V_HEADS + which * GDN_V_HEADS + 2 * kh + j)
    return jnp.array(idx, jnp.int32)


def _ssd_params(j, ssd_w_in, ssd_conv_w, ssd_conv_b, ssd_a_log, ssd_dt_bias, ssd_d):
    order = _ssd_small_order()
    w = ssd_w_in[j]
    per_row = lambda v: v.reshape(-1)[order].reshape(SSD_GROUPS, 2 * SSD_HEADS_PER_GROUP)
    return dict(
        w_main=w[:, :SSD_MAIN].astype(BF16),
        w_small_t=w[:, SSD_MAIN:][:, order].T.astype(BF16),
        conv_w=ssd_conv_w[j], conv_b=ssd_conv_b[j].reshape(1, -1),
        bias=_lanes(per_row(ssd_dt_bias[j])), alog=_lanes(per_row(ssd_a_log[j])),
        dskip=jnp.repeat(ssd_d[j].astype(F32), SSD_HEADDIM).reshape(1, SSD_INNER),
    )


def _gdn_params(j, gdn_w_in, gdn_conv_w, gdn_conv_b, gdn_a_log, gdn_dt_bias):
    order = _gdn_small_order()
    w = gdn_w_in[j]
    def per_row(v):
        full = jnp.stack([jnp.zeros_like(v), v], axis=1).astype(F32)
        return full.reshape(-1)[order].reshape(GDN_K_HEADS, 8)
    return dict(
        w_main=w[:, :GDN_MAIN].astype(BF16),
        w_small_t=w[:, GDN_MAIN:][:, order].T.astype(BF16),
        conv_w=gdn_conv_w[j], conv_b=gdn_conv_b[j].reshape(1, -1),
        bias=_lanes(per_row(gdn_dt_bias[j])), alog=_lanes(per_row(gdn_a_log[j])),
    )


def _ssd_state_in(s):
    n = s.shape[0]
    s = s.reshape(n, 2, SSD_GROUPS, SSD_PAIRS_PER_GROUP, 2, SSD_HEADDIM, SSD_STATE)
    return s.transpose(0, 1, 2, 3, 6, 4, 5).reshape(n, 2, SSD_GROUPS, SSD_PAIRS_PER_GROUP, SSD_STATE, LANES)


def _ssd_state_out(s):
    n = s.shape[0]
    s = s.reshape(n, 2, SSD_GROUPS, SSD_PAIRS_PER_GROUP, SSD_STATE, 2, SSD_HEADDIM)
    return s.transpose(0, 1, 2, 3, 5, 6, 4).reshape(n, 2, SSD_HEADS, SSD_HEADDIM, SSD_STATE)


def kernel(x_prompt, x_sample, state_ssd, state_gdn, c, c_ctx, w_ada, b_ada, norm_w, final_norm_w, ssd_w_in, ssd_conv_w, ssd_conv_b, ssd_a_log, ssd_dt_bias, ssd_d, ssd_norm_w, ssd_w_out, gdn_w_in, gdn_conv_w, gdn_conv_b, gdn_a_log, gdn_dt_bias, gdn_norm_w, gdn_w_out, w_router, router_bias, moe_w_gate, moe_w_up, moe_w_down):
    nb, lc, d = x_prompt.shape
    ns, ls, _ = x_sample.shape
    depth = w_ada.shape[0]
    tc, ts = nb * lc, ns * ls
    assert tc % ls == 0 and ns + 1 <= N_MOD_ROWS

    x = _embed(x_prompt, x_sample)
    cond = jnp.concatenate([c_ctx[None, :], c, jnp.zeros((N_MOD_ROWS - 1 - ns, d), F32)], axis=0)
    mods = _ada(cond, w_ada, b_ada).reshape(depth, N_MOD_ROWS, 6, d)

    seg_big = _SegOf(_tile(1024, tc, ls), tc, ls)
    seg_mid = _SegOf(_tile(512, tc, ls), tc, ls)
    w_router_t = w_router.T.astype(F32)
    ssd_states, gdn_states = [], []
    for i in range(depth):
        j = i // 2
        if i % 2 == 0:
            prm = _ssd_params(j, ssd_w_in, ssd_conv_w, ssd_conv_b, ssd_a_log, ssd_dt_bias, ssd_d)
            p_main, p_small_t = _proj(x, mods, i, seg_big, norm_w[i, 0], prm["w_main"], prm["w_small_t"])
            rows = p_small_t.reshape(SSD_GROUPS, 2 * SSD_HEADS_PER_GROUP, tc + ts)
            y, s_ctx = _ssd_scan(p_main, rows, prm, None, nb, lc, 0, True)
            y, _ = _ssd_scan(p_main, rows, prm, _ssd_state_in(state_ssd[:, j].astype(F32)), ns, ls,
                             tc // ls, False, y_prev=y)
            ssd_states.append(_ssd_state_out(s_ctx))
            x, hf, gates = _post(y, p_main, 0, x, mods, i, seg_mid, ssd_norm_w[j], ssd_w_out[j].astype(BF16),
                                 norm_w[i, 1], w_router_t, router_bias, SSD_GROUP_WIDTH, True)
        else:
            prm = _gdn_params(j, gdn_w_in, gdn_conv_w, gdn_conv_b, gdn_a_log, gdn_dt_bias)
            p_main, p_small_t = _proj(x, mods, i, seg_big, norm_w[i, 0], prm["w_main"], prm["w_small_t"])
            rows = p_small_t.reshape(GDN_K_HEADS, 8, tc + ts)
            y, s_ctx = _gdn_scan(p_main, rows, prm, None, nb, lc, 0, True)
            y, _ = _gdn_scan(p_main, rows, prm, state_gdn[:, j:j + 1].astype(F32), ns, ls, tc // ls, False,
                             o_prev=y)
            gdn_states.append(s_ctx[:, 0])
            x, hf, gates = _post(y, p_main, GDN_CONV_DIM // GDN_VAL_DIM, x, mods, i, seg_mid,
                                 jnp.tile(gdn_norm_w[j], GDN_V_HEADS), gdn_w_out[j].astype(BF16),
                                 norm_w[i, 1], w_router_t, router_bias, GDN_DV, False)
        outs = _moe(hf, gates, moe_w_gate[i].astype(BF16), moe_w_up[i].astype(BF16), moe_w_down[i].astype(BF16),
                    x, mods, i, seg_big, final_norm_w, i == depth - 1)
        x = outs[0]
    y_prompt = outs[0].reshape(nb, lc, d)
    y_sample = outs[1].reshape(ns, ls, d)
    return (y_prompt, y_sample, jnp.stack(ssd_states, axis=1), jnp.stack(gdn_states, axis=1))
```

```python
import functools

import jax
import jax.numpy as jnp
from jax import lax
from jax.experimental import pallas as pl
from jax.experimental.pallas import tpu as pltpu

F32 = jnp.float32
BF16 = jnp.bfloat16
HIGHEST = lax.Precision.HIGHEST

D_MODEL = 1024
GRID_W = 64
RMS_EPS = 1e-6

SSD_INNER = 2 * D_MODEL
SSD_HEADDIM = 64
SSD_HEADS = SSD_INNER // SSD_HEADDIM
SSD_GROUPS = 4
SSD_STATE = 128
SSD_HEADS_PER_GROUP = SSD_HEADS // SSD_GROUPS
SSD_PAIRS_PER_GROUP = SSD_HEADS_PER_GROUP // 2
SSD_GROUP_WIDTH = SSD_INNER // SSD_GROUPS
SSD_CONV_DIM = SSD_INNER + 2 * SSD_GROUPS * SSD_STATE
SSD_MAIN = SSD_INNER + SSD_CONV_DIM

GDN_K_HEADS = 8
GDN_V_HEADS = 16
GDN_DK = 128
GDN_DV = 128
GDN_KEY_DIM = GDN_K_HEADS * GDN_DK
GDN_VAL_DIM = GDN_V_HEADS * GDN_DV
GDN_CONV_DIM = 2 * GDN_KEY_DIM + GDN_VAL_DIM
GDN_MAIN = GDN_CONV_DIM + GDN_VAL_DIM

N_EXPERTS = 16
EXPERTS_PER_GROUP = 4
N_EXPERT_GROUPS = N_EXPERTS // EXPERTS_PER_GROUP
D_EXPERT = 512

LANES = 128
CHUNK = 128
GDN_PREP_CHUNKS = 4
MOE_EXPERTS_PER_STEP = 2
assert N_EXPERTS % MOE_EXPERTS_PER_STEP == 0
MOE_ROWS = 160
assert GDN_DK == CHUNK
BF16_ROWS = 16
N_MOD_ROWS = 16
VMEM_LIMIT = 48 * 1024 * 1024
MOE_VMEM_LIMIT = 56 * 1024 * 1024


def _cparams(sem, vmem_limit=VMEM_LIMIT):
    return pltpu.CompilerParams(dimension_semantics=sem, vmem_limit_bytes=vmem_limit)


def _tile(target, *dims):
    t = target
    while any(d % t for d in dims):
        t //= 2
    assert t >= 8
    return t


def _silu(x):
    return x * jax.nn.sigmoid(x)


def _softplus(x):
    return jnp.maximum(x, 0.0) + jnp.log1p(jnp.exp(-jnp.abs(x)))


def _rms_unit(x):
    return x * lax.rsqrt(jnp.mean(x * x, axis=-1, keepdims=True) + RMS_EPS)


def _dot(a, b, **kw):
    return jnp.dot(a, b, preferred_element_type=F32, **kw)


def _dot_nt(a, b, **kw):
    return lax.dot_general(a, b, (((1,), (1,)), ((), ())), preferred_element_type=F32, **kw)


def _dot_tn(a, b):
    return lax.dot_general(a, b, (((0,), (0,)), ((), ())), preferred_element_type=F32)


def _embed_kernel(xp_ref, xs_ref, pos_ref, o_ref, *, n_ctx_tiles):
    i = pl.program_id(0)

    @pl.when(i < n_ctx_tiles)
    def _():
        o_ref[...] = xp_ref[...]

    @pl.when(i >= n_ctx_tiles)
    def _():
        o_ref[...] = xs_ref[...] + pos_ref[...]


def _grid_pos_embed(n_tokens, d):
    t = jnp.arange(n_tokens)
    row = (t // GRID_W).astype(F32)
    col = (t % GRID_W).astype(F32)
    quarter = d // 4
    omega = 1.0 / (10000.0 ** (jnp.arange(quarter, dtype=F32) / quarter))

    def emb(pos):
        ang = pos[:, None] * omega[None, :]
        return jnp.concatenate([jnp.sin(ang), jnp.cos(ang)], axis=-1)

    return jnp.concatenate([emb(row), emb(col)], axis=-1)


def _embed(x_prompt, x_sample):
    nb, lc, d = x_prompt.shape
    ns, ls, _ = x_sample.shape
    tc, ts = nb * lc, ns * ls
    tm = _tile(1024, tc, ls)
    nct, tiles_per_seq = tc // tm, ls // tm
    pos = _grid_pos_embed((ls // GRID_W) * GRID_W, d)
    return pl.pallas_call(
        functools.partial(_embed_kernel, n_ctx_tiles=nct),
        grid=((tc + ts) // tm,),
        in_specs=[
            pl.BlockSpec((tm, d), lambda i: (jnp.minimum(i, nct - 1), 0)),
            pl.BlockSpec((tm, d), lambda i: (jnp.maximum(i - nct, 0), 0)),
            pl.BlockSpec((tm, d), lambda i: (jnp.maximum(i - nct, 0) % tiles_per_seq, 0)),
        ],
        out_specs=pl.BlockSpec((tm, d), lambda i: (i, 0)),
        out_shape=jax.ShapeDtypeStruct((tc + ts, d), F32),
        compiler_params=_cparams(("arbitrary",)),
        name="embed",
    )(x_prompt.reshape(tc, d), x_sample.reshape(ts, d), pos)


def _ada_kernel(c_ref, w_ref, b_ref, o_ref):
    s = _silu(c_ref[...])
    o_ref[...] = _dot(s, w_ref[...], precision=HIGHEST) + b_ref[...]


def _ada(cond, w_ada, b_ada):
    depth, d, six_d = w_ada.shape
    return pl.pallas_call(
        _ada_kernel,
        grid=(depth, six_d // d),
        in_specs=[
            pl.BlockSpec((N_MOD_ROWS, d), lambda l, j: (0, 0)),
            pl.BlockSpec((None, d, d), lambda l, j: (l, 0, j)),
            pl.BlockSpec((None, 1, d), lambda l, j: (l, 0, j)),
        ],
        out_specs=pl.BlockSpec((None, N_MOD_ROWS, d), lambda l, j: (l, 0, j)),
        out_shape=jax.ShapeDtypeStruct((depth, N_MOD_ROWS, six_d), F32),
        compiler_params=_cparams(("arbitrary", "arbitrary")),
        name="ada",
    )(cond, w_ada, b_ada.reshape(depth, 1, six_d))


def _proj_kernel(x_ref, mod_ref, nw_ref, w_ref, wst_ref, o_ref, ost_ref, hm_ref):
    @pl.when(pl.program_id(1) == 0)
    def _():
        hm = _rms_unit(x_ref[...]) * nw_ref[...] * (1.0 + mod_ref[1:2, :]) + mod_ref[0:1, :]
        hmb = hm.astype(BF16)
        hm_ref[...] = hmb
        ost_ref[...] = _dot_nt(wst_ref[...], hmb)

    o_ref[...] = _dot(hm_ref[...], w_ref[...]).astype(BF16)


def _proj(x, mods, layer, seg_of, norm_w, w_main, w_small_t):
    t, d = x.shape
    n_main = w_main.shape[1]
    n_small = w_small_t.shape[0]
    tm = seg_of.tile
    tn = _tile(1024, n_main)
    return pl.pallas_call(
        _proj_kernel,
        grid=(t // tm, n_main // tn),
        in_specs=[
            pl.BlockSpec((tm, d), lambda i, j: (i, 0)),
            pl.BlockSpec((None, None, 6, d), lambda i, j: (layer, seg_of(i), 0, 0)),
            pl.BlockSpec((1, d), lambda i, j: (0, 0)),
            pl.BlockSpec((d, tn), lambda i, j: (0, j)),
            pl.BlockSpec((n_small, d), lambda i, j: (0, 0)),
        ],
        out_specs=[
            pl.BlockSpec((tm, tn), lambda i, j: (i, j)),
            pl.BlockSpec((n_small, tm), lambda i, j: (0, i)),
        ],
        out_shape=[
            jax.ShapeDtypeStruct((t, n_main), BF16),
            jax.ShapeDtypeStruct((n_small, t), F32),
        ],
        scratch_shapes=[pltpu.VMEM((tm, d), BF16)],
        compiler_params=_cparams(("arbitrary", "arbitrary")),
        name="proj",
    )(x, mods, norm_w.reshape(1, d), w_main, w_small_t)


class _SegOf:
    def __init__(self, tile, n_ctx_tokens, sample_len):
        assert n_ctx_tokens % tile == 0 and sample_len % tile == 0
        self.tile = tile
        self.n_ctx_tiles = n_ctx_tokens // tile
        self.tiles_per_sample = sample_len // tile

    def __call__(self, i):
        return jnp.where(i < self.n_ctx_tiles, 0, 1 + (i - self.n_ctx_tiles) // self.tiles_per_sample)


def _chunk_consts():
    ii = lax.broadcasted_iota(jnp.int32, (CHUNK, CHUNK), 0)
    jj = lax.broadcasted_iota(jnp.int32, (CHUNK, CHUNK), 1)
    return ii, jj


def _conv_load(src_ref, c, n_chunks):
    seq_len = n_chunks * CHUNK
    r0 = pl.multiple_of(c * CHUNK, CHUNK)
    lo = pl.multiple_of(jnp.maximum(r0 - BF16_ROWS, 0), BF16_ROWS)
    hi = pl.multiple_of(jnp.minimum(r0 + CHUNK, seq_len - BF16_ROWS), BF16_ROWS)
    return (src_ref[pl.ds(r0, CHUNK), :], src_ref[pl.ds(lo, BF16_ROWS), :], src_ref[pl.ds(hi, BF16_ROWS), :])


def _conv_apply(loaded, c, n_chunks, w, b):
    xb, before, after = (t.astype(F32) for t in loaded)
    prev_row = before[BF16_ROWS - 1:BF16_ROWS, :] * jnp.where(c > 0, 1.0, 0.0)
    next_row = after[0:1, :] * jnp.where(c < n_chunks - 1, 1.0, 0.0)
    row = lax.broadcasted_iota(jnp.int32, xb.shape, 0)
    xp = jnp.where(row == 0, prev_row, pltpu.roll(xb, 1, 0))
    xn = jnp.where(row == CHUNK - 1, next_row, pltpu.roll(xb, CHUNK - 1, 0))
    y = w[0:1, :] * xp + w[1:2, :] * xb + w[2:3, :] * xn + b
    return _silu(y)


def _conv_silu(src_ref, c, n_chunks, w_ref, b_ref):
    return _conv_apply(_conv_load(src_ref, c, n_chunks), c, n_chunks, w_ref[...], b_ref[...])


def _rows_to_cols(*row_blocks):
    n = sum(b.shape[0] for b in row_blocks)
    pad = jnp.zeros((LANES - n, CHUNK), F32)
    return jnp.concatenate(list(row_blocks) + [pad], axis=0).T


def _ssd_kernel(*refs, n_chunks, has_s0, has_prev, emit_state):
    it = iter(refs)
    x_ref, b_ref, c_ref, rows_ref, bias_ref, alog_ref = (next(it) for _ in range(6))
    cwx_ref, cwb_ref, cwc_ref, cbx_ref, cbb_ref, cbc_ref, dsk_ref = (next(it) for _ in range(7))
    s0_ref = next(it) if has_s0 else None
    if has_prev:
        next(it)
    y_ref = next(it)
    sout_ref = next(it) if emit_state else None
    xc_ref, cc_ref, bt_ref, g_ref, cols_ref, arow_ref, lrow_ref, part_ref, st_ref = (next(it) for _ in range(9))

    ii, jj = _chunk_consts()
    lane = lax.broadcasted_iota(jnp.int32, (CHUNK, LANES), 1)
    first_head = lane < SSD_HEADDIM
    nh = SSD_HEADS_PER_GROUP
    spread = (lax.broadcasted_iota(jnp.int32, (LANES, SSD_GROUP_WIDTH), 1) // SSD_HEADDIM
              == lax.broadcasted_iota(jnp.int32, (LANES, SSD_GROUP_WIDTH), 0)).astype(BF16)
    keep = ((ii >= jj), (ii <= jj))
    cum = ((ii <= jj).astype(F32), (ii >= jj).astype(F32))
    edge = (CHUNK - 1, 0)

    if has_s0:
        st_ref[...] = s0_ref[...]
    else:
        st_ref[...] = jnp.zeros(st_ref.shape, F32)

    def prepare(p, carry):
        res = []
        for cc in range(2):
            c = 2 * p + cc
            rows = pl.ds(pl.multiple_of(c * CHUNK, CHUNK), CHUNK)
            xc = _conv_silu(x_ref, c, n_chunks, cwx_ref, cbx_ref).astype(BF16)
            bc = _conv_silu(b_ref, c, n_chunks, cwb_ref, cbb_ref)
            cb = _conv_silu(c_ref, c, n_chunks, cwc_ref, cbc_ref).astype(BF16)
            per_dir = []
            for d in range(2):
                sl = slice(nh * d, nh * d + nh)
                dt = _softplus(rows_ref[sl, rows] + bias_ref[sl, :])
                acs = _dot(dt * (-jnp.exp(alog_ref[sl, :])), cum[d], precision=HIGHEST)
                per_dir.append((acs, jnp.log(dt) - acs, _rows_to_cols(acs, dt)))
            res.append((rows, xc, cb, bc.T.astype(BF16), _dot_nt(cb, bc.astype(BF16)), per_dir))
        for rows, xc, cb, bt, g, per_dir in res:
            xc_ref[rows, :] = xc
            cc_ref[rows, :] = cb
            bt_ref[rows, :] = bt
            g_ref[rows, :] = g
            for d, (acs, lrow, cols) in enumerate(per_dir):
                arow_ref[d, :, rows] = acs
                lrow_ref[d, :, rows] = lrow
                cols_ref[d, rows, :] = cols
        return carry

    def trip(t, second_half):
        cidx = (t, n_chunks - 1 - t)
        rows = [pl.ds(pl.multiple_of(c * CHUNK, CHUNK), CHUNK) for c in cidx]
        chains = [(d, p) for d in range(2) for p in range(SSD_PAIRS_PER_GROUP)]
        st = {ch: st_ref[ch[0], ch[1]] for ch in chains}
        xcb = [xc_ref[r, :] for r in rows]
        cb = [cc_ref[r, :] for r in rows]
        bt = [bt_ref[r, :] for r in rows]
        g = [g_ref[r, :] for r in rows]
        cols = [cols_ref[d, rows[d], :] for d in range(2)]
        arow = [arow_ref[d, :, rows[d]] for d in range(2)]
        lrow = [lrow_ref[d, :, rows[d]] for d in range(2)]
        zero = jnp.zeros((CHUNK, LANES), BF16)
        intra, inter = {}, {}
        for d, p in chains:
            xpb = xcb[d][:, LANES * p:LANES * (p + 1)]
            ms = []
            for k in range(2):
                h = 2 * p + k
                decay = jnp.exp(jnp.where(keep[d], cols[d][:, h:h + 1] + lrow[d][h:h + 1, :], -jnp.inf))
                ms.append((g[d] * decay).astype(BF16))
            xblk = jnp.concatenate([jnp.where(first_head, xpb, zero), jnp.where(first_head, zero, xpb)], axis=0)
            intra[(d, p)] = _dot(jnp.concatenate(ms, axis=1), xblk)
            inter[(d, p)] = _dot(cb[d], st[(d, p)].astype(BF16))
        ea, wx = [], []
        for d in range(2):
            last_row = cols[d][edge[d]:edge[d] + 1, :]
            dt_cols = pltpu.roll(cols[d], LANES - nh, 1)
            ea.append(_dot(jnp.exp(cols[d]).astype(BF16), spread))
            wx.append(_dot((jnp.exp(last_row - cols[d]) * dt_cols).astype(BF16), spread))
        ys, new = {}, {}
        for d, p in chains:
            h0, h1 = 2 * p, 2 * p + 1
            pair = slice(LANES * p, LANES * (p + 1))
            last = jnp.where(first_head[0:1, :], arow[d][h0:h0 + 1, edge[d]:edge[d] + 1],
                             arow[d][h1:h1 + 1, edge[d]:edge[d] + 1])
            ys[(d, p)] = intra[(d, p)] + inter[(d, p)] * ea[d][:, pair]
            xt = (xcb[d][:, pair].astype(F32) * wx[d][:, pair]).astype(BF16)
            new[(d, p)] = st[(d, p)] * jnp.exp(last) + _dot(bt[d], xt)
        for d, p in chains:
            st_ref[d, p] = new[(d, p)]
        for d in range(2):
            y = jnp.concatenate([ys[(d, p)] for p in range(SSD_PAIRS_PER_GROUP)], axis=1)
            if second_half:
                skip = dsk_ref[...] * xcb[d].astype(F32)
                y_ref[rows[d], :] = (part_ref[rows[d], :] + y + skip).astype(BF16)
            else:
                part_ref[rows[d], :] = y

    def first_body(t, carry):
        trip(t, False)
        return carry

    def second_body(t, carry):
        trip(t, True)
        return carry

    lax.fori_loop(0, n_chunks // 2, prepare, 0)
    lax.fori_loop(0, n_chunks // 2, first_body, 0)
    lax.fori_loop(n_chunks // 2, n_chunks, second_body, 0)
    if emit_state:
        sout_ref[...] = st_ref[...]


def _ssd_scan(p_main, rows, prm, s0, n_seq, seq_len, row_block0, emit_state, y_prev=None):
    t = p_main.shape[0]
    gw = SSD_GROUP_WIDTH
    x0 = SSD_INNER // gw
    b0 = (2 * SSD_INNER) // SSD_STATE
    c0 = b0 + SSD_GROUPS
    cx0 = 0
    cb0 = SSD_INNER // SSD_STATE
    cc0 = cb0 + SSD_GROUPS
    has_s0 = s0 is not None
    rb = lambda b: row_block0 + b
    st_block = (None, 2, None, SSD_PAIRS_PER_GROUP, SSD_STATE, LANES)
    st_map = lambda b, g: (b, 0, g, 0, 0, 0)
    in_specs = [
        pl.BlockSpec((seq_len, gw), lambda b, g: (rb(b), x0 + g)),
        pl.BlockSpec((seq_len, SSD_STATE), lambda b, g: (rb(b), b0 + g)),
        pl.BlockSpec((seq_len, SSD_STATE), lambda b, g: (rb(b), c0 + g)),
        pl.BlockSpec((None, 2 * SSD_HEADS_PER_GROUP, seq_len), lambda b, g: (g, 0, rb(b))),
        pl.BlockSpec((None, 2 * SSD_HEADS_PER_GROUP, LANES), lambda b, g: (g, 0, 0)),
        pl.BlockSpec((None, 2 * SSD_HEADS_PER_GROUP, LANES), lambda b, g: (g, 0, 0)),
        pl.BlockSpec((3, gw), lambda b, g: (0, cx0 + g)),
        pl.BlockSpec((3, SSD_STATE), lambda b, g: (0, cb0 + g)),
        pl.BlockSpec((3, SSD_STATE), lambda b, g: (0, cc0 + g)),
        pl.BlockSpec((1, gw), lambda b, g: (0, cx0 + g)),
        pl.BlockSpec((1, SSD_STATE), lambda b, g: (0, cb0 + g)),
        pl.BlockSpec((1, SSD_STATE), lambda b, g: (0, cc0 + g)),
        pl.BlockSpec((1, gw), lambda b, g: (0, g)),
    ]
    args = [p_main, p_main, p_main, rows, prm["bias"], prm["alog"],
            prm["conv_w"], prm["conv_w"], prm["conv_w"], prm["conv_b"], prm["conv_b"], prm["conv_b"], prm["dskip"]]
    if has_s0:
        in_specs.append(pl.BlockSpec(st_block, st_map))
        args.append(s0)
    aliases = {}
    if y_prev is not None:
        in_specs.append(pl.BlockSpec(memory_space=pl.ANY))
        args.append(y_prev)
        aliases = {len(args) - 1: 0}
    out_specs = [pl.BlockSpec((seq_len, gw), lambda b, g: (rb(b), g))]
    out_shape = [jax.ShapeDtypeStruct((t, SSD_INNER), BF16)]
    if emit_state:
        out_specs.append(pl.BlockSpec(st_block, st_map))
        out_shape.append(jax.ShapeDtypeStruct(
            (n_seq, 2, SSD_GROUPS, SSD_PAIRS_PER_GROUP, SSD_STATE, LANES), F32))
    assert t % seq_len == 0 and seq_len % (2 * CHUNK) == 0 and SSD_STATE == CHUNK
    outs = pl.pallas_call(
        functools.partial(_ssd_kernel, n_chunks=seq_len // CHUNK, has_s0=has_s0, has_prev=y_prev is not None,
                          emit_state=emit_state),
        grid=(n_seq, SSD_GROUPS),
        in_specs=in_specs,
        out_specs=out_specs,
        out_shape=out_shape,
        input_output_aliases=aliases,
        scratch_shapes=[
            pltpu.VMEM((seq_len, gw), BF16),
            pltpu.VMEM((seq_len, SSD_STATE), BF16),
            pltpu.VMEM((seq_len, CHUNK), BF16),
            pltpu.VMEM((seq_len, CHUNK), F32),
            pltpu.VMEM((2, seq_len, LANES), F32),
            pltpu.VMEM((2, SSD_HEADS_PER_GROUP, seq_len), F32),
            pltpu.VMEM((2, SSD_HEADS_PER_GROUP, seq_len), F32),
            pltpu.VMEM((seq_len, gw), F32),
            pltpu.VMEM((2, SSD_PAIRS_PER_GROUP, SSD_STATE, LANES), F32),
        ],
        compiler_params=_cparams(("arbitrary", "arbitrary")),
        name="ssd_scan",
    )(*args)
    return outs if emit_state else (outs[0], None)


def _merge_masks(ii, jj):
    masks = []
    s = 1
    while s < CHUNK:
        masks.append(((ii // s) != (jj // s)) & ((ii // (2 * s)) == (jj // (2 * s))))
        s *= 2
    return masks


def _gdn_kernel(*refs, n_chunks, has_s0, has_prev, emit_state):
    it = iter(refs)
    q_ref, k_ref, v_ref, rows_ref, bias_ref, alog_ref = (next(it) for _ in range(6))
    cwq_ref, cwk_ref, cwv_ref, cbq_ref, cbk_ref, cbv_ref = (next(it) for _ in range(6))
    s0_ref = next(it) if has_s0 else None
    if has_prev:
        next(it)
    o_ref = next(it)
    sout_ref = next(it) if emit_state else None
    a_ref, b_ref, c_ref, d_ref, el_ref, part_ref, mk_ref, qq_ref, kt_ref, rhs_ref, st_ref = (
        next(it) for _ in range(11))

    ii, jj = _chunk_consts()
    eye = (ii == jj).astype(F32)
    masks = _merge_masks(ii, jj)
    for lvl, mask in enumerate(masks[1:]):
        mk_ref[lvl] = mask.astype(BF16)
    keep = ((ii >= jj), (ii <= jj))
    cum = ((ii <= jj).astype(F32), (ii >= jj).astype(F32))
    edge = (CHUNK - 1, 0)
    chains = [(d, j) for d in range(2) for j in range(2)]

    if has_s0:
        st_ref[...] = s0_ref[...]
    else:
        st_ref[...] = jnp.zeros(st_ref.shape, F32)

    prep = min(GDN_PREP_CHUNKS, n_chunks)
    assert n_chunks % prep == 0

    def setup(p):
        work = []
        cw = [r[...] for r in (cwq_ref, cwk_ref, cwv_ref)]
        cb = [r[...] for r in (cbq_ref, cbk_ref, cbv_ref)]
        alog, bias = alog_ref[...], bias_ref[...]
        loaded = []
        for cc in range(prep):
            c = p * prep + cc
            rows = pl.ds(pl.multiple_of(c * CHUNK, CHUNK), CHUNK)
            loaded.append(([_conv_load(r, c, n_chunks) for r in (q_ref, k_ref, v_ref)], rows_ref[:, rows]))
        for cc in range(prep):
            c = p * prep + cc
            (lq, lk, lv), raw = loaded[cc]
            qc = _conv_apply(lq, c, n_chunks, cw[0], cb[0])
            kc = _conv_apply(lk, c, n_chunks, cw[1], cb[1])
            vc = _conv_apply(lv, c, n_chunks, cw[2], cb[2])
            qn = qc * (lax.rsqrt(jnp.sum(qc * qc, axis=-1, keepdims=True) + RMS_EPS) * (GDN_DK ** -0.5))
            kn = kc * lax.rsqrt(jnp.sum(kc * kc, axis=-1, keepdims=True) + RMS_EPS)
            knt = kn.T
            kntb = knt.astype(BF16)
            kk = _dot(kn.astype(BF16), kntb)
            qk = _dot(qn.astype(BF16), kntb)
            beta = jax.nn.sigmoid(raw)
            gl = -jnp.exp(alog) * _softplus(raw + bias)
            gcum = [_dot(gl, cum[d], precision=HIGHEST) for d in range(2)]
            cols = _rows_to_cols(beta, gcum[0], gcum[1])
            for d in range(2):
                for j in range(2):
                    rg = 4 + 2 * d + j
                    g_r = gcum[d][rg:rg + 1, :]
                    b_f = jnp.broadcast_to(cols[:, 2 * d + j:2 * d + j + 1], (CHUNK, LANES))
                    g_f = jnp.broadcast_to(cols[:, 8 + 8 * d + rg:8 + 8 * d + rg + 1], (CHUNK, LANES))
                    dec = jnp.exp(jnp.where(keep[d], g_f - g_r, -jnp.inf))
                    m = kk * dec * b_f
                    eg = jnp.exp(g_f)
                    last = gcum[d][rg:rg + 1, edge[d]:edge[d] + 1]
                    ch = 2 * d + j
                    el_ref[ch, c] = jnp.broadcast_to(jnp.exp(last), (1, LANES))
                    i = len(work)
                    qq_ref[i] = jnp.concatenate([qn * eg, qk * dec], axis=1).astype(BF16)
                    kt_ref[i] = (knt * jnp.exp(last - g_r)).astype(BF16)
                    rhs_ref[i] = jnp.concatenate([vc[:, GDN_DV * j:GDN_DV * (j + 1)] * b_f, kn * (b_f * eg)], axis=1)
                    work.append(dict(mb=m.astype(BF16), tinv=eye - jnp.where(masks[0], m, 0.0)))
        return work

    def finish(p, work, level_masks):
        for mk in level_masks:
            for wk in work:
                tb = wk["tinv"].astype(BF16)
                y = _dot(tb, wk["mb"] * mk)
                wk["tinv"] = wk["tinv"] - _dot(y.astype(BF16), tb)
        solb = []
        for i, wk in enumerate(work):
            rhs = rhs_ref[i]
            sol = rhs + _dot((wk["tinv"] - eye).astype(BF16), rhs.astype(BF16))
            solb.append(sol.astype(BF16))
        res = []
        for i in range(len(work)):
            qq = qq_ref[i]
            kq = _dot(jnp.concatenate([kt_ref[i], qq[:, GDN_DK:]], axis=0), solb[i])
            ka, qa = kq[:GDN_DK], kq[GDN_DK:]
            res.append((ka[:, GDN_DV:].astype(BF16), ka[:, :GDN_DV],
                        (qq[:, :GDN_DK].astype(F32) - qa[:, GDN_DV:]).astype(BF16), qa[:, :GDN_DV].astype(BF16)))
        for i, (a, b, c_, d_) in enumerate(res):
            rows = pl.ds(pl.multiple_of((p * prep + i // 4) * CHUNK, CHUNK), CHUNK)
            a_ref[i % 4, rows, :] = a
            b_ref[i % 4, rows, :] = b
            c_ref[i % 4, rows, :] = c_
            d_ref[i % 4, rows, :] = d_

    def prepare(p, carry):
        level_masks = [mk_ref[lvl] for lvl in range(len(masks) - 1)]
        finish(p, setup(p), level_masks)
        return carry

    def trip(t, second_half):
        cidx = (t, n_chunks - 1 - t)
        rows = [pl.ds(pl.multiple_of(c * CHUNK, CHUNK), CHUNK) for c in cidx]
        st = [st_ref[d, j] for d, j in chains]
        sb = [s.astype(BF16) for s in st]
        new = [st[2 * d + j] * el_ref[2 * d + j, cidx[d]] + b_ref[2 * d + j, rows[d], :]
               - _dot(a_ref[2 * d + j, rows[d], :], sb[2 * d + j]) for d, j in chains]
        outs = [_dot(c_ref[2 * d + j, rows[d], :], sb[2 * d + j]) + d_ref[2 * d + j, rows[d], :].astype(F32)
                for d, j in chains]
        for d, j in chains:
            st_ref[d, j] = new[2 * d + j]
        for d in range(2):
            o = jnp.concatenate([outs[2 * d], outs[2 * d + 1]], axis=1)
            if second_half:
                o_ref[rows[d], :] = (part_ref[rows[d], :] + o).astype(BF16)
            else:
                part_ref[rows[d], :] = o

    def first_body(t, carry):
        trip(t, False)
        return carry

    def second_body(t, carry):
        trip(t, True)
        return carry

    lax.fori_loop(0, n_chunks // prep, prepare, 0)
    lax.fori_loop(0, n_chunks // 2, first_body, 0)
    lax.fori_loop(n_chunks // 2, n_chunks, second_body, 0)
    if emit_state:
        sout_ref[...] = st_ref[...]


def _gdn_scan(p_main, rows, prm, s0, n_seq, seq_len, row_block0, emit_state, o_prev=None):
    vw = 2 * GDN_DV
    k0 = GDN_KEY_DIM // GDN_DK
    v0 = (2 * GDN_KEY_DIM) // vw
    has_s0 = s0 is not None
    rb = lambda b: row_block0 + b
    st_block = (None, None, 2, 2, GDN_DK, GDN_DV)
    st_map = lambda b, h: (b, 0, 0, h, 0, 0)
    in_specs = [
        pl.BlockSpec((seq_len, GDN_DK), lambda b, h: (rb(b), h)),
        pl.BlockSpec((seq_len, GDN_DK), lambda b, h: (rb(b), k0 + h)),
        pl.BlockSpec((seq_len, vw), lambda b, h: (rb(b), v0 + h)),
        pl.BlockSpec((None, 8, seq_len), lambda b, h: (h, 0, rb(b))),
        pl.BlockSpec((None, 8, LANES), lambda b, h: (h, 0, 0)),
        pl.BlockSpec((None, 8, LANES), lambda b, h: (h, 0, 0)),
        pl.BlockSpec((3, GDN_DK), lambda b, h: (0, h)),
        pl.BlockSpec((3, GDN_DK), lambda b, h: (0, k0 + h)),
        pl.BlockSpec((3, vw), lambda b, h: (0, v0 + h)),
        pl.BlockSpec((1, GDN_DK), lambda b, h: (0, h)),
        pl.BlockSpec((1, GDN_DK), lambda b, h: (0, k0 + h)),
        pl.BlockSpec((1, vw), lambda b, h: (0, v0 + h)),
    ]
    args = [p_main, p_main, p_main, rows, prm["bias"], prm["alog"],
            prm["conv_w"], prm["conv_w"], prm["conv_w"], prm["conv_b"], prm["conv_b"], prm["conv_b"]]
    if has_s0:
        in_specs.append(pl.BlockSpec(st_block, st_map))
        args.append(s0)
    aliases = {}
    if o_prev is not None:
        in_specs.append(pl.BlockSpec(memory_space=pl.ANY))
        args.append(o_prev)
        aliases = {len(args) - 1: 0}
    out_specs = [pl.BlockSpec((seq_len, vw), lambda b, h: (rb(b), h))]
    out_shape = [jax.ShapeDtypeStruct((p_main.shape[0], GDN_VAL_DIM), BF16)]
    if emit_state:
        out_specs.append(pl.BlockSpec(st_block, st_map))
        out_shape.append(jax.ShapeDtypeStruct((n_seq, 1, 2, GDN_V_HEADS, GDN_DK, GDN_DV), F32))
    assert seq_len % (2 * CHUNK) == 0
    outs = pl.pallas_call(
        functools.partial(_gdn_kernel, n_chunks=seq_len // CHUNK, has_s0=has_s0, has_prev=o_prev is not None,
                          emit_state=emit_state),
        grid=(n_seq, GDN_K_HEADS),
        in_specs=in_specs,
        out_specs=out_specs,
        out_shape=out_shape,
        input_output_aliases=aliases,
        scratch_shapes=[
            pltpu.VMEM((4, seq_len, GDN_DK), BF16),
            pltpu.VMEM((4, seq_len, GDN_DV), F32),
            pltpu.VMEM((4, seq_len, GDN_DK), BF16),
            pltpu.VMEM((4, seq_len, GDN_DV), BF16),
            pltpu.VMEM((4, seq_len // CHUNK, 1, LANES), F32),
            pltpu.VMEM((seq_len, vw), F32),
            pltpu.VMEM((CHUNK.bit_length() - 2, CHUNK, CHUNK), BF16),
            pltpu.VMEM((4 * GDN_PREP_CHUNKS, CHUNK, GDN_DK + CHUNK), BF16),
            pltpu.VMEM((4 * GDN_PREP_CHUNKS, GDN_DK, CHUNK), BF16),
            pltpu.VMEM((4 * GDN_PREP_CHUNKS, CHUNK, GDN_DV + GDN_DK), F32),
            pltpu.VMEM((2, 2, GDN_DK, GDN_DV), F32),
        ],
        compiler_params=_cparams(("arbitrary", "arbitrary")),
        name="gdn_scan",
    )(*args)
    return outs if emit_state else (outs[0], None)


def _route(logits_t, rbias):
    aff = jax.nn.sigmoid(logits_t)
    sel = aff + rbias
    a = [aff[e:e + 1, :] for e in range(N_EXPERTS)]
    s = [sel[e:e + 1, :] for e in range(N_EXPERTS)]
    gs = []
    for g in range(N_EXPERT_GROUPS):
        v = s[EXPERTS_PER_GROUP * g:EXPERTS_PER_GROUP * (g + 1)]
        best = None
        for x in range(EXPERTS_PER_GROUP):
            for y in range(x + 1, EXPERTS_PER_GROUP):
                ps = v[x] + v[y]
                best = ps if best is None else jnp.maximum(best, ps)
        gs.append(best)
    top, gi = gs[0], jnp.zeros_like(gs[0], dtype=jnp.int32)
    for g in range(1, N_EXPERT_GROUPS):
        up = gs[g] > top
        top = jnp.where(up, gs[g], top)
        gi = jnp.where(up, g, gi)
    ms = [jnp.where(gi == e // EXPERTS_PER_GROUP, s[e], -jnp.inf) for e in range(N_EXPERTS)]
    b1, i1 = ms[0], jnp.zeros_like(gi)
    for e in range(1, N_EXPERTS):
        up = ms[e] > b1
        b1 = jnp.where(up, ms[e], b1)
        i1 = jnp.where(up, e, i1)
    b2, i2 = jnp.full_like(b1, -jnp.inf), jnp.zeros_like(gi)
    for e in range(N_EXPERTS):
        cand = jnp.where(i1 == e, -jnp.inf, ms[e])
        up = cand > b2
        b2 = jnp.where(up, cand, b2)
        i2 = jnp.where(up, e, i2)
    w1 = sum(jnp.where(i1 == e, a[e], 0.0) for e in range(N_EXPERTS))
    w2 = sum(jnp.where(i2 == e, a[e], 0.0) for e in range(N_EXPERTS))
    tot = w1 + w2
    rows = [jnp.where(i1 == e, w1, 0.0) / tot + jnp.where(i2 == e, w2, 0.0) / tot for e in range(N_EXPERTS)]
    return jnp.concatenate(rows, axis=0)


def _post_kernel(y_ref, z_ref, nw_ref, wo_ref, x_ref, mod_ref, n2_ref, wrt_ref, rb_ref,
                 x1_ref, hf_ref, gate_ref, *, group_width, gate_before_norm):
    y = y_ref[...].astype(F32)
    gz = _silu(z_ref[...].astype(F32))
    if gate_before_norm:
        y = y * gz
    parts = [_rms_unit(y[:, s:s + group_width]) for s in range(0, y.shape[1], group_width)]
    yn = jnp.concatenate(parts, axis=1) * nw_ref[...]
    if not gate_before_norm:
        yn = yn * gz
    out = _dot(yn.astype(BF16), wo_ref[...])
    x1 = x_ref[...] + mod_ref[2:3, :] * out
    x1_ref[...] = x1
    hf = _rms_unit(x1) * n2_ref[...] * (1.0 + mod_ref[4:5, :]) + mod_ref[3:4, :]
    hf_ref[...] = hf.astype(BF16)
    logits_t = _dot_nt(wrt_ref[...], hf, precision=HIGHEST)
    gate_ref[...] = _route(logits_t, rb_ref[...])


def _post(y, p_main, z_block0, x, mods, layer, seg_of, norm_w_full, w_out, norm2_w, w_router_t, router_bias,
          group_width, gate_before_norm):
    t, d = x.shape
    inner = y.shape[1]
    tm = seg_of.tile
    return pl.pallas_call(
        functools.partial(_post_kernel, group_width=group_width, gate_before_norm=gate_before_norm),
        grid=(t // tm,),
        in_specs=[
            pl.BlockSpec((tm, inner), lambda i: (i, 0)),
            pl.BlockSpec((tm, inner), lambda i: (i, z_block0)),
            pl.BlockSpec((1, inner), lambda i: (0, 0)),
            pl.BlockSpec((inner, d), lambda i: (0, 0)),
            pl.BlockSpec((tm, d), lambda i: (i, 0)),
            pl.BlockSpec((None, None, 6, d), lambda i: (layer, seg_of(i), 0, 0)),
            pl.BlockSpec((1, d), lambda i: (0, 0)),
            pl.BlockSpec((N_EXPERTS, d), lambda i: (0, 0)),
            pl.BlockSpec((N_EXPERTS, 1), lambda i: (0, 0)),
        ],
        out_specs=[
            pl.BlockSpec((tm, d), lambda i: (i, 0)),
            pl.BlockSpec((tm, d), lambda i: (i, 0)),
            pl.BlockSpec((N_EXPERTS, tm), lambda i: (0, i)),
        ],
        out_shape=[
            jax.ShapeDtypeStruct((t, d), F32),
            jax.ShapeDtypeStruct((t, d), BF16),
            jax.ShapeDtypeStruct((N_EXPERTS, t), F32),
        ],
        compiler_params=_cparams(("arbitrary",)),
        name="post",
    )(y, p_main, norm_w_full.reshape(1, inner), w_out, x, mods, norm2_w.reshape(1, d), w_router_t,
      router_bias.reshape(N_EXPERTS, 1))


def _moe_kernel(hf_ref, gate_ref, tri_ref, wg_ref, wu_ref, wd_ref, x1_ref, mod_ref, fn_ref, o_ref,
                acc_ref, rank_ref, cnt_ref, *, final_norm, n_ctx_tiles):
    e = pl.program_id(1)
    tm = hf_ref.shape[0]

    @pl.when(e == 0)
    def _():
        acc_ref[...] = jnp.zeros(acc_ref.shape, F32)
        sel = jnp.where(gate_ref[...] > 0.0, 1.0, 0.0)
        rank_ref[...] = _dot(sel.astype(BF16), tri_ref[...])
        for k in range(N_EXPERTS):
            cnt_ref[k] = jnp.sum(sel[k:k + 1, :]).astype(jnp.int32)

    sub = lax.broadcasted_iota(jnp.int32, (MOE_ROWS, tm), 0).astype(F32)
    for k in range(MOE_EXPERTS_PER_STEP):
        ex = e * MOE_EXPERTS_PER_STEP + k
        g_row = gate_ref[pl.ds(ex, 1), :]
        slot = jnp.where(g_row > 0.0, rank_ref[pl.ds(ex, 1), :], -1.0)

        def block(b, carry, k=k, g_row=g_row, slot=slot):
            pf = jnp.where(sub == slot - (b * MOE_ROWS).astype(F32), 1.0, 0.0)
            pb = pf.astype(BF16)
            xg = _dot(pb, hf_ref[...]).astype(BF16)
            hid = _silu(_dot(xg, wg_ref[k])) * _dot(xg, wu_ref[k])
            y = _dot(hid.astype(BF16), wd_ref[k])
            gsel = jnp.sum(pf * g_row, axis=1, keepdims=True)
            acc_ref[...] += _dot_tn(pb, (y * gsel).astype(BF16))
            return carry

        lax.fori_loop(0, (cnt_ref[ex] + MOE_ROWS - 1) // MOE_ROWS, block, 0)

    @pl.when(e == pl.num_programs(1) - 1)
    def _():
        x2 = x1_ref[...] + mod_ref[5:6, :] * acc_ref[...]
        if not final_norm:
            o_ref[0][...] = x2
        else:
            y = _rms_unit(x2) * fn_ref[...]
            i = pl.program_id(0)

            @pl.when(i < n_ctx_tiles)
            def _():
                o_ref[0][...] = y

            @pl.when(i >= n_ctx_tiles)
            def _():
                o_ref[1][...] = y


def _moe_kernel_entry(*refs, final_norm, n_ctx_tiles):
    n_out = 2 if final_norm else 1
    _moe_kernel(*refs[:9], refs[9:9 + n_out], *refs[9 + n_out:], final_norm=final_norm, n_ctx_tiles=n_ctx_tiles)


def _moe(hf, gates, wg, wu, wd, x1, mods, layer, seg_of, final_norm_w, final_norm):
    t, d = x1.shape
    n_e, _, de = wg.shape
    tm = seg_of.tile
    nct = seg_of.n_ctx_tiles
    earlier = jnp.triu(jnp.ones((tm, tm), BF16), k=1)
    if final_norm:
        out_specs = [pl.BlockSpec((tm, d), lambda i, e: (jnp.minimum(i, nct - 1), 0)),
                     pl.BlockSpec((tm, d), lambda i, e: (jnp.maximum(i - nct, 0), 0))]
        out_shape = [jax.ShapeDtypeStruct((nct * tm, d), F32), jax.ShapeDtypeStruct((t - nct * tm, d), F32)]
    else:
        out_specs = [pl.BlockSpec((tm, d), lambda i, e: (i, 0))]
        out_shape = [jax.ShapeDtypeStruct((t, d), F32)]
    return pl.pallas_call(
        functools.partial(_moe_kernel_entry, final_norm=final_norm, n_ctx_tiles=nct),
        grid=(t // tm, n_e // MOE_EXPERTS_PER_STEP),
        in_specs=[
            pl.BlockSpec((tm, d), lambda i, e: (i, 0)),
            pl.BlockSpec((N_EXPERTS, tm), lambda i, e: (0, i)),
            pl.BlockSpec((tm, tm), lambda i, e: (0, 0)),
            pl.BlockSpec((MOE_EXPERTS_PER_STEP, d, de), lambda i, e: (e, 0, 0)),
            pl.BlockSpec((MOE_EXPERTS_PER_STEP, d, de), lambda i, e: (e, 0, 0)),
            pl.BlockSpec((MOE_EXPERTS_PER_STEP, de, d), lambda i, e: (e, 0, 0)),
            pl.BlockSpec((tm, d), lambda i, e: (i, 0)),
            pl.BlockSpec((None, None, 6, d), lambda i, e: (layer, seg_of(i), 0, 0)),
            pl.BlockSpec((1, d), lambda i, e: (0, 0)),
        ],
        out_specs=out_specs,
        out_shape=out_shape,
        scratch_shapes=[
            pltpu.VMEM((tm, d), F32),
            pltpu.VMEM((N_EXPERTS, tm), F32),
            pltpu.SMEM((N_EXPERTS,), jnp.int32),
        ],
        compiler_params=_cparams(("arbitrary", "arbitrary"), MOE_VMEM_LIMIT),
        name="moe",
    )(hf, gates, earlier, wg, wu, wd, x1, mods, final_norm_w.reshape(1, d))


def _lanes(v):
    return jnp.broadcast_to(v[..., None].astype(F32), v.shape + (LANES,))


def _ssd_small_order():
    idx = []
    for g in range(SSD_GROUPS):
        for d in range(2):
            for h in range(SSD_HEADS_PER_GROUP):
                idx.append(d * SSD_HEADS + SSD_HEADS_PER_GROUP * g + h)
    return jnp.array(idx, jnp.int32)


def _gdn_small_order():
    idx = []
    for kh in range(GDN_K_HEADS):
        for which in range(2):
            for d in range(2):
                for j in range(2):
                    idx.append(d * 2 * GDN_V_HEADS + which * GDN_V_HEADS + 2 * kh + j)
    return jnp.array(idx, jnp.int32)


def _ssd_params(j, ssd_w_in, ssd_conv_w, ssd_conv_b, ssd_a_log, ssd_dt_bias, ssd_d):
    order = _ssd_small_order()
    w = ssd_w_in[j]
    per_row = lambda v: v.reshape(-1)[order].reshape(SSD_GROUPS, 2 * SSD_HEADS_PER_GROUP)
    return dict(
        w_main=w[:, :SSD_MAIN].astype(BF16),
        w_small_t=w[:, SSD_MAIN:][:, order].T.astype(BF16),
        conv_w=ssd_conv_w[j], conv_b=ssd_conv_b[j].reshape(1, -1),
        bias=_lanes(per_row(ssd_dt_bias[j])), alog=_lanes(per_row(ssd_a_log[j])),
        dskip=jnp.repeat(ssd_d[j].astype(F32), SSD_HEADDIM).reshape(1, SSD_INNER),
    )


def _gdn_params(j, gdn_w_in, gdn_conv_w, gdn_conv_b, gdn_a_log, gdn_dt_bias):
    order = _gdn_small_order()
    w = gdn_w_in[j]
    def per_row(v):
        full = jnp.stack([jnp.zeros_like(v), v], axis=1).astype(F32)
        return full.reshape(-1)[order].reshape(GDN_K_HEADS, 8)
    return dict(
        w_main=w[:, :GDN_MAIN].astype(BF16),
        w_small_t=w[:, GDN_MAIN:][:, order].T.astype(BF16),
        conv_w=gdn_conv_w[j], conv_b=gdn_conv_b[j].reshape(1, -1),
        bias=_lanes(per_row(gdn_dt_bias[j])), alog=_lanes(per_row(gdn_a_log[j])),
    )


def _ssd_state_in(s):
    n = s.shape[0]
    s = s.reshape(n, 2, SSD_GROUPS, SSD_PAIRS_PER_GROUP, 2, SSD_HEADDIM, SSD_STATE)
    return s.transpose(0, 1, 2, 3, 6, 4, 5).reshape(n, 2, SSD_GROUPS, SSD_PAIRS_PER_GROUP, SSD_STATE, LANES)


def _ssd_state_out(s):
    n = s.shape[0]
    s = s.reshape(n, 2, SSD_GROUPS, SSD_PAIRS_PER_GROUP, SSD_STATE, 2, SSD_HEADDIM)
    return s.transpose(0, 1, 2, 3, 5, 6, 4).reshape(n, 2, SSD_HEADS, SSD_HEADDIM, SSD_STATE)


def kernel(x_prompt, x_sample, state_ssd, state_gdn, c, c_ctx, w_ada, b_ada, norm_w, final_norm_w, ssd_w_in, ssd_conv_w, ssd_conv_b, ssd_a_log, ssd_dt_bias, ssd_d, ssd_norm_w, ssd_w_out, gdn_w_in, gdn_conv_w, gdn_conv_b, gdn_a_log, gdn_dt_bias, gdn_norm_w, gdn_w_out, w_router, router_bias, moe_w_gate, moe_w_up, moe_w_down):
    nb, lc, d = x_prompt.shape
    ns, ls, _ = x_sample.shape
    depth = w_ada.shape[0]
    tc, ts = nb * lc, ns * ls
    assert tc % ls == 0 and ns + 1 <= N_MOD_ROWS

    x = _embed(x_prompt, x_sample)
    cond = jnp.concatenate([c_ctx[None, :], c, jnp.zeros((N_MOD_ROWS - 1 - ns, d), F32)], axis=0)
    mods = _ada(cond, w_ada, b_ada).reshape(depth, N_MOD_ROWS, 6, d)

    seg_big = _SegOf(_tile(1024, tc, ls), tc, ls)
    seg_mid = _SegOf(_tile(512, tc, ls), tc, ls)
    w_router_t = w_router.T.astype(F32)
    ssd_states, gdn_states = [], []
    for i in range(depth):
        j = i // 2
        if i % 2 == 0:
            prm = _ssd_params(j, ssd_w_in, ssd_conv_w, ssd_conv_b, ssd_a_log, ssd_dt_bias, ssd_d)
            p_main, p_small_t = _proj(x, mods, i, seg_big, norm_w[i, 0], prm["w_main"], prm["w_small_t"])
            rows = p_small_t.reshape(SSD_GROUPS, 2 * SSD_HEADS_PER_GROUP, tc + ts)
            y, s_ctx = _ssd_scan(p_main, rows, prm, None, nb, lc, 0, True)
            y, _ = _ssd_scan(p_main, rows, prm, _ssd_state_in(state_ssd[:, j].astype(F32)), ns, ls,
                             tc // ls, False, y_prev=y)
            ssd_states.append(_ssd_state_out(s_ctx))
            x, hf, gates = _post(y, p_main, 0, x, mods, i, seg_mid, ssd_norm_w[j], ssd_w_out[j].astype(BF16),
                                 norm_w[i, 1], w_router_t, router_bias, SSD_GROUP_WIDTH, True)
        else:
            prm = _gdn_params(j, gdn_w_in, gdn_conv_w, gdn_conv_b, gdn_a_log, gdn_dt_bias)
            p_main, p_small_t = _proj(x, mods, i, seg_big, norm_w[i, 0], prm["w_main"], prm["w_small_t"])
            rows = p_small_t.reshape(GDN_K_HEADS, 8, tc + ts)
            y, s_ctx = _gdn_scan(p_main, rows, prm, None, nb, lc, 0, True)
            y, _ = _gdn_scan(p_main, rows, prm, state_gdn[:, j:j + 1].astype(F32), ns, ls, tc // ls, False,
                             o_prev=y)
            gdn_states.append(s_ctx[:, 0])
            x, hf, gates = _post(y, p_main, GDN_CONV_DIM // GDN_VAL_DIM, x, mods, i, seg_mid,
                                 jnp.tile(gdn_norm_w[j], GDN_V_HEADS), gdn_w_out[j].astype(BF16),
                                 norm_w[i, 1], w_router_t, router_bias, GDN_DV, False)
        outs = _moe(hf, gates, moe_w_gate[i].astype(BF16), moe_w_up[i].astype(BF16), moe_w_down[i].astype(BF16),
                    x, mods, i, seg_big, final_norm_w, i == depth - 1)
        x = outs[0]
    y_prompt = outs[0].reshape(nb, lc, d)
    y_sample = outs[1].reshape(ns, ls, d)
    return (y_prompt, y_sample, jnp.stack(ssd_states, axis=1), jnp.stack(gdn_states, axis=1))
```

```python
import functools

import jax
import jax.numpy as jnp
from jax import lax
from jax.experimental import pallas as pl
from jax.experimental.pallas import tpu as pltpu

F32 = jnp.float32
BF16 = jnp.bfloat16
HIGHEST = lax.Precision.HIGHEST

D_MODEL = 1024
GRID_W = 64
RMS_EPS = 1e-6

SSD_INNER = 2 * D_MODEL
SSD_HEADDIM = 64
SSD_HEADS = SSD_INNER // SSD_HEADDIM
SSD_GROUPS = 4
SSD_STATE = 128
SSD_HEADS_PER_GROUP = SSD_HEADS // SSD_GROUPS
SSD_PAIRS_PER_GROUP = SSD_HEADS_PER_GROUP // 2
SSD_GROUP_WIDTH = SSD_INNER // SSD_GROUPS
SSD_CONV_DIM = SSD_INNER + 2 * SSD_GROUPS * SSD_STATE
SSD_MAIN = SSD_INNER + SSD_CONV_DIM

GDN_K_HEADS = 8
GDN_V_HEADS = 16
GDN_DK = 128
GDN_DV = 128
GDN_KEY_DIM = GDN_K_HEADS * GDN_DK
GDN_VAL_DIM = GDN_V_HEADS * GDN_DV
GDN_CONV_DIM = 2 * GDN_KEY_DIM + GDN_VAL_DIM
GDN_MAIN = GDN_CONV_DIM + GDN_VAL_DIM

N_EXPERTS = 16
EXPERTS_PER_GROUP = 4
N_EXPERT_GROUPS = N_EXPERTS // EXPERTS_PER_GROUP
D_EXPERT = 512

LANES = 128
CHUNK = 128
GDN_PREP_CHUNKS = 8
PROJ_COLS = 1024
MOE_EXPERTS_PER_STEP = 2
assert N_EXPERTS % MOE_EXPERTS_PER_STEP == 0
MOE_ROWS = 160
assert GDN_DK == CHUNK
BF16_ROWS = 16
N_MOD_ROWS = 16
VMEM_LIMIT = 48 * 1024 * 1024
MOE_VMEM_LIMIT = 56 * 1024 * 1024


def _cparams(sem, vmem_limit=VMEM_LIMIT):
    return pltpu.CompilerParams(dimension_semantics=sem, vmem_limit_bytes=vmem_limit)


def _tile(target, *dims):
    t = target
    while any(d % t for d in dims):
        t //= 2
    assert t >= 8
    return t


def _silu(x):
    return x * jax.nn.sigmoid(x)


def _softplus(x):
    return jnp.maximum(x, 0.0) + jnp.log1p(jnp.exp(-jnp.abs(x)))


def _rms_unit(x):
    return x * lax.rsqrt(jnp.mean(x * x, axis=-1, keepdims=True) + RMS_EPS)


def _dot(a, b, **kw):
    return jnp.dot(a, b, preferred_element_type=F32, **kw)


def _dot_nt(a, b, **kw):
    return lax.dot_general(a, b, (((1,), (1,)), ((), ())), preferred_element_type=F32, **kw)


def _dot_tn(a, b):
    return lax.dot_general(a, b, (((0,), (0,)), ((), ())), preferred_element_type=F32)


def _embed_kernel(xp_ref, xs_ref, pos_ref, o_ref, *, n_ctx_tiles):
    i = pl.program_id(0)

    @pl.when(i < n_ctx_tiles)
    def _():
        o_ref[...] = xp_ref[...]

    @pl.when(i >= n_ctx_tiles)
    def _():
        o_ref[...] = xs_ref[...] + pos_ref[...]


def _grid_pos_embed(n_tokens, d):
    t = jnp.arange(n_tokens)
    row = (t // GRID_W).astype(F32)
    col = (t % GRID_W).astype(F32)
    quarter = d // 4
    omega = 1.0 / (10000.0 ** (jnp.arange(quarter, dtype=F32) / quarter))

    def emb(pos):
        ang = pos[:, None] * omega[None, :]
        return jnp.concatenate([jnp.sin(ang), jnp.cos(ang)], axis=-1)

    return jnp.concatenate([emb(row), emb(col)], axis=-1)


def _embed(x_prompt, x_sample):
    nb, lc, d = x_prompt.shape
    ns, ls, _ = x_sample.shape
    tc, ts = nb * lc, ns * ls
    tm = _tile(1024, tc, ls)
    nct, tiles_per_seq = tc // tm, ls // tm
    pos = _grid_pos_embed((ls // GRID_W) * GRID_W, d)
    return pl.pallas_call(
        functools.partial(_embed_kernel, n_ctx_tiles=nct),
        grid=((tc + ts) // tm,),
        in_specs=[
            pl.BlockSpec((tm, d), lambda i: (jnp.minimum(i, nct - 1), 0)),
            pl.BlockSpec((tm, d), lambda i: (jnp.maximum(i - nct, 0), 0)),
            pl.BlockSpec((tm, d), lambda i: (jnp.maximum(i - nct, 0) % tiles_per_seq, 0)),
        ],
        out_specs=pl.BlockSpec((tm, d), lambda i: (i, 0)),
        out_shape=jax.ShapeDtypeStruct((tc + ts, d), F32),
        compiler_params=_cparams(("arbitrary",)),
        name="embed",
    )(x_prompt.reshape(tc, d), x_sample.reshape(ts, d), pos)


def _ada_kernel(c_ref, w_ref, b_ref, o_ref):
    s = _silu(c_ref[...])
    o_ref[...] = _dot(s, w_ref[...], precision=HIGHEST) + b_ref[...]


def _ada(cond, w_ada, b_ada):
    depth, d, six_d = w_ada.shape
    return pl.pallas_call(
        _ada_kernel,
        grid=(depth, six_d // d),
        in_specs=[
            pl.BlockSpec((N_MOD_ROWS, d), lambda l, j: (0, 0)),
            pl.BlockSpec((None, d, d), lambda l, j: (l, 0, j)),
            pl.BlockSpec((None, 1, d), lambda l, j: (l, 0, j)),
        ],
        out_specs=pl.BlockSpec((None, N_MOD_ROWS, d), lambda l, j: (l, 0, j)),
        out_shape=jax.ShapeDtypeStruct((depth, N_MOD_ROWS, six_d), F32),
        compiler_params=_cparams(("arbitrary", "arbitrary")),
        name="ada",
    )(cond, w_ada, b_ada.reshape(depth, 1, six_d))


def _proj_kernel(x_ref, mod_ref, nw_ref, w_ref, wst_ref, o_ref, ost_ref):
    hm = _rms_unit(x_ref[...]) * nw_ref[...] * (1.0 + mod_ref[1:2, :]) + mod_ref[0:1, :]
    hmb = hm.astype(BF16)
    ost_ref[...] = _dot_nt(wst_ref[...], hmb)
    tn = PROJ_COLS
    for j in range(w_ref.shape[1] // tn):
        o_ref[:, j * tn:(j + 1) * tn] = _dot(hmb, w_ref[:, j * tn:(j + 1) * tn]).astype(BF16)


def _proj(x, mods, layer, seg_of, norm_w, w_main, w_small_t):
    t, d = x.shape
    n_main = w_main.shape[1]
    n_small = w_small_t.shape[0]
    tm = seg_of.tile
    assert n_main % PROJ_COLS == 0
    return pl.pallas_call(
        _proj_kernel,
        grid=(t // tm,),
        in_specs=[
            pl.BlockSpec((tm, d), lambda i: (i, 0)),
            pl.BlockSpec((None, None, 6, d), lambda i: (layer, seg_of(i), 0, 0)),
            pl.BlockSpec((1, d), lambda i: (0, 0)),
            pl.BlockSpec((d, n_main), lambda i: (0, 0)),
            pl.BlockSpec((n_small, d), lambda i: (0, 0)),
        ],
        out_specs=[
            pl.BlockSpec((tm, n_main), lambda i: (i, 0)),
            pl.BlockSpec((n_small, tm), lambda i: (0, i)),
        ],
        out_shape=[
            jax.ShapeDtypeStruct((t, n_main), BF16),
            jax.ShapeDtypeStruct((n_small, t), F32),
        ],
        compiler_params=_cparams(("arbitrary",)),
        name="proj",
    )(x, mods, norm_w.reshape(1, d), w_main, w_small_t)


class _SegOf:
    def __init__(self, tile, n_ctx_tokens, sample_len):
        assert n_ctx_tokens % tile == 0 and sample_len % tile == 0
        self.tile = tile
        self.n_ctx_tiles = n_ctx_tokens // tile
        self.tiles_per_sample = sample_len // tile

    def __call__(self, i):
        return jnp.where(i < self.n_ctx_tiles, 0, 1 + (i - self.n_ctx_tiles) // self.tiles_per_sample)


def _chunk_consts():
    ii = lax.broadcasted_iota(jnp.int32, (CHUNK, CHUNK), 0)
    jj = lax.broadcasted_iota(jnp.int32, (CHUNK, CHUNK), 1)
    return ii, jj


def _conv_load(src_ref, c, n_chunks):
    seq_len = n_chunks * CHUNK
    r0 = pl.multiple_of(c * CHUNK, CHUNK)
    lo = pl.multiple_of(jnp.maximum(r0 - BF16_ROWS, 0), BF16_ROWS)
    hi = pl.multiple_of(jnp.minimum(r0 + CHUNK, seq_len - BF16_ROWS), BF16_ROWS)
    return (src_ref[pl.ds(r0, CHUNK), :], src_ref[pl.ds(lo, BF16_ROWS), :], src_ref[pl.ds(hi, BF16_ROWS), :])


def _conv_apply(loaded, c, n_chunks, w, b):
    xb, before, after = (t.astype(F32) for t in loaded)
    prev_row = before[BF16_ROWS - 1:BF16_ROWS, :] * jnp.where(c > 0, 1.0, 0.0)
    next_row = after[0:1, :] * jnp.where(c < n_chunks - 1, 1.0, 0.0)
    row = lax.broadcasted_iota(jnp.int32, xb.shape, 0)
    xp = jnp.where(row == 0, prev_row, pltpu.roll(xb, 1, 0))
    xn = jnp.where(row == CHUNK - 1, next_row, pltpu.roll(xb, CHUNK - 1, 0))
    y = w[0:1, :] * xp + w[1:2, :] * xb + w[2:3, :] * xn + b
    return _silu(y)


def _conv_silu(src_ref, c, n_chunks, w_ref, b_ref):
    return _conv_apply(_conv_load(src_ref, c, n_chunks), c, n_chunks, w_ref[...], b_ref[...])


def _rows_to_cols(*row_blocks):
    n = sum(b.shape[0] for b in row_blocks)
    pad = jnp.zeros((LANES - n, CHUNK), F32)
    return jnp.concatenate(list(row_blocks) + [pad], axis=0).T


def _ssd_kernel(*refs, n_chunks, has_s0, has_prev, emit_state):
    it = iter(refs)
    x_ref, b_ref, c_ref, rows_ref, bias_ref, alog_ref = (next(it) for _ in range(6))
    cwx_ref, cwb_ref, cwc_ref, cbx_ref, cbb_ref, cbc_ref, dsk_ref = (next(it) for _ in range(7))
    s0_ref = next(it) if has_s0 else None
    if has_prev:
        next(it)
    y_ref = next(it)
    sout_ref = next(it) if emit_state else None
    xc_ref, cc_ref, bt_ref, g_ref, cols_ref, arow_ref, lrow_ref, part_ref, st_ref = (next(it) for _ in range(9))

    ii, jj = _chunk_consts()
    lane = lax.broadcasted_iota(jnp.int32, (CHUNK, LANES), 1)
    first_head = lane < SSD_HEADDIM
    nh = SSD_HEADS_PER_GROUP
    spread = (lax.broadcasted_iota(jnp.int32, (LANES, SSD_GROUP_WIDTH), 1) // SSD_HEADDIM
              == lax.broadcasted_iota(jnp.int32, (LANES, SSD_GROUP_WIDTH), 0)).astype(BF16)
    keep = ((ii >= jj), (ii <= jj))
    cum = ((ii <= jj).astype(F32), (ii >= jj).astype(F32))
    edge = (CHUNK - 1, 0)

    if has_s0:
        st_ref[...] = s0_ref[...]
    else:
        st_ref[...] = jnp.zeros(st_ref.shape, F32)

    def prepare(p, carry):
        res = []
        for cc in range(2):
            c = 2 * p + cc
            rows = pl.ds(pl.multiple_of(c * CHUNK, CHUNK), CHUNK)
            xc = _conv_silu(x_ref, c, n_chunks, cwx_ref, cbx_ref).astype(BF16)
            bc = _conv_silu(b_ref, c, n_chunks, cwb_ref, cbb_ref)
            cb = _conv_silu(c_ref, c, n_chunks, cwc_ref, cbc_ref).astype(BF16)
            per_dir = []
            for d in range(2):
                sl = slice(nh * d, nh * d + nh)
                dt = _softplus(rows_ref[sl, rows] + bias_ref[sl, :])
                acs = _dot(dt * (-jnp.exp(alog_ref[sl, :])), cum[d], precision=HIGHEST)
                per_dir.append((acs, jnp.log(dt) - acs, _rows_to_cols(acs, dt)))
            res.append((rows, xc, cb, bc.T.astype(BF16), _dot_nt(cb, bc.astype(BF16)), per_dir))
        for rows, xc, cb, bt, g, per_dir in res:
            xc_ref[rows, :] = xc
            cc_ref[rows, :] = cb
            bt_ref[rows, :] = bt
            g_ref[rows, :] = g
            for d, (acs, lrow, cols) in enumerate(per_dir):
                arow_ref[d, :, rows] = acs
                lrow_ref[d, :, rows] = lrow
                cols_ref[d, rows, :] = cols
        return carry

    def trip(t, second_half):
        cidx = (t, n_chunks - 1 - t)
        rows = [pl.ds(pl.multiple_of(c * CHUNK, CHUNK), CHUNK) for c in cidx]
        chains = [(d, p) for d in range(2) for p in range(SSD_PAIRS_PER_GROUP)]
        st = {ch: st_ref[ch[0], ch[1]] for ch in chains}
        xcb = [xc_ref[r, :] for r in rows]
        cb = [cc_ref[r, :] for r in rows]
        bt = [bt_ref[r, :] for r in rows]
        g = [g_ref[r, :] for r in rows]
        cols = [cols_ref[d, rows[d], :] for d in range(2)]
        arow = [arow_ref[d, :, rows[d]] for d in range(2)]
        lrow = [lrow_ref[d, :, rows[d]] for d in range(2)]
        zero = jnp.zeros((CHUNK, LANES), BF16)
        intra, inter = {}, {}
        for d, p in chains:
            xpb = xcb[d][:, LANES * p:LANES * (p + 1)]
            ms = []
            for k in range(2):
                h = 2 * p + k
                decay = jnp.exp(jnp.where(keep[d], cols[d][:, h:h + 1] + lrow[d][h:h + 1, :], -jnp.inf))
                ms.append((g[d] * decay).astype(BF16))
            xblk = jnp.concatenate([jnp.where(first_head, xpb, zero), jnp.where(first_head, zero, xpb)], axis=0)
            intra[(d, p)] = _dot(jnp.concatenate(ms, axis=1), xblk)
            inter[(d, p)] = _dot(cb[d], st[(d, p)].astype(BF16))
        ea, wx = [], []
        for d in range(2):
            last_row = cols[d][edge[d]:edge[d] + 1, :]
            dt_cols = pltpu.roll(cols[d], LANES - nh, 1)
            ea.append(_dot(jnp.exp(cols[d]).astype(BF16), spread))
            wx.append(_dot((jnp.exp(last_row - cols[d]) * dt_cols).astype(BF16), spread))
        ys, new = {}, {}
        for d, p in chains:
            h0, h1 = 2 * p, 2 * p + 1
            pair = slice(LANES * p, LANES * (p + 1))
            last = jnp.where(first_head[0:1, :], arow[d][h0:h0 + 1, edge[d]:edge[d] + 1],
                             arow[d][h1:h1 + 1, edge[d]:edge[d] + 1])
            ys[(d, p)] = intra[(d, p)] + inter[(d, p)] * ea[d][:, pair]
            xt = (xcb[d][:, pair].astype(F32) * wx[d][:, pair]).astype(BF16)
            new[(d, p)] = st[(d, p)] * jnp.exp(last) + _dot(bt[d], xt)
        for d, p in chains:
            st_ref[d, p] = new[(d, p)]
        for d in range(2):
            y = jnp.concatenate([ys[(d, p)] for p in range(SSD_PAIRS_PER_GROUP)], axis=1)
            if second_half:
                skip = dsk_ref[...] * xcb[d].astype(F32)
                y_ref[rows[d], :] = (part_ref[rows[d], :] + y + skip).astype(BF16)
            else:
                part_ref[rows[d], :] = y

    def first_body(t, carry):
        trip(t, False)
        return carry

    def second_body(t, carry):
        trip(t, True)
        return carry

    lax.fori_loop(0, n_chunks // 2, prepare, 0)
    lax.fori_loop(0, n_chunks // 2, first_body, 0)
    lax.fori_loop(n_chunks // 2, n_chunks, second_body, 0)
    if emit_state:
        sout_ref[...] = st_ref[...]


def _ssd_scan(p_main, rows, prm, s0, n_seq, seq_len, row_block0, emit_state, y_prev=None):
    t = p_main.shape[0]
    gw = SSD_GROUP_WIDTH
    x0 = SSD_INNER // gw
    b0 = (2 * SSD_INNER) // SSD_STATE
    c0 = b0 + SSD_GROUPS
    cx0 = 0
    cb0 = SSD_INNER // SSD_STATE
    cc0 = cb0 + SSD_GROUPS
    has_s0 = s0 is not None
    rb = lambda b: row_block0 + b
    st_block = (None, 2, None, SSD_PAIRS_PER_GROUP, SSD_STATE, LANES)
    st_map = lambda b, g: (b, 0, g, 0, 0, 0)
    in_specs = [
        pl.BlockSpec((seq_len, gw), lambda b, g: (rb(b), x0 + g)),
        pl.BlockSpec((seq_len, SSD_STATE), lambda b, g: (rb(b), b0 + g)),
        pl.BlockSpec((seq_len, SSD_STATE), lambda b, g: (rb(b), c0 + g)),
        pl.BlockSpec((None, 2 * SSD_HEADS_PER_GROUP, seq_len), lambda b, g: (g, 0, rb(b))),
        pl.BlockSpec((None, 2 * SSD_HEADS_PER_GROUP, LANES), lambda b, g: (g, 0, 0)),
        pl.BlockSpec((None, 2 * SSD_HEADS_PER_GROUP, LANES), lambda b, g: (g, 0, 0)),
        pl.BlockSpec((3, gw), lambda b, g: (0, cx0 + g)),
        pl.BlockSpec((3, SSD_STATE), lambda b, g: (0, cb0 + g)),
        pl.BlockSpec((3, SSD_STATE), lambda b, g: (0, cc0 + g)),
        pl.BlockSpec((1, gw), lambda b, g: (0, cx0 + g)),
        pl.BlockSpec((1, SSD_STATE), lambda b, g: (0, cb0 + g)),
        pl.BlockSpec((1, SSD_STATE), lambda b, g: (0, cc0 + g)),
        pl.BlockSpec((1, gw), lambda b, g: (0, g)),
    ]
    args = [p_main, p_main, p_main, rows, prm["bias"], prm["alog"],
            prm["conv_w"], prm["conv_w"], prm["conv_w"], prm["conv_b"], prm["conv_b"], prm["conv_b"], prm["dskip"]]
    if has_s0:
        in_specs.append(pl.BlockSpec(st_block, st_map))
        args.append(s0)
    aliases = {}
    if y_prev is not None:
        in_specs.append(pl.BlockSpec(memory_space=pl.ANY))
        args.append(y_prev)
        aliases = {len(args) - 1: 0}
    out_specs = [pl.BlockSpec((seq_len, gw), lambda b, g: (rb(b), g))]
    out_shape = [jax.ShapeDtypeStruct((t, SSD_INNER), BF16)]
    if emit_state:
        out_specs.append(pl.BlockSpec(st_block, st_map))
        out_shape.append(jax.ShapeDtypeStruct(
            (n_seq, 2, SSD_GROUPS, SSD_PAIRS_PER_GROUP, SSD_STATE, LANES), F32))
    assert t % seq_len == 0 and seq_len % (2 * CHUNK) == 0 and SSD_STATE == CHUNK
    outs = pl.pallas_call(
        functools.partial(_ssd_kernel, n_chunks=seq_len // CHUNK, has_s0=has_s0, has_prev=y_prev is not None,
                          emit_state=emit_state),
        grid=(n_seq, SSD_GROUPS),
        in_specs=in_specs,
        out_specs=out_specs,
        out_shape=out_shape,
        input_output_aliases=aliases,
        scratch_shapes=[
            pltpu.VMEM((seq_len, gw), BF16),
            pltpu.VMEM((seq_len, SSD_STATE), BF16),
            pltpu.VMEM((seq_len, CHUNK), BF16),
            pltpu.VMEM((seq_len, CHUNK), F32),
            pltpu.VMEM((2, seq_len, LANES), F32),
            pltpu.VMEM((2, SSD_HEADS_PER_GROUP, seq_len), F32),
            pltpu.VMEM((2, SSD_HEADS_PER_GROUP, seq_len), F32),
            pltpu.VMEM((seq_len, gw), F32),
            pltpu.VMEM((2, SSD_PAIRS_PER_GROUP, SSD_STATE, LANES), F32),
        ],
        compiler_params=_cparams(("arbitrary", "arbitrary")),
        name="ssd_scan",
    )(*args)
    return outs if emit_state else (outs[0], None)


def _merge_masks(ii, jj):
    masks = []
    s = 1
    while s < CHUNK:
        masks.append(((ii // s) != (jj // s)) & ((ii // (2 * s)) == (jj // (2 * s))))
        s *= 2
    return masks


def _gdn_kernel(*refs, n_chunks, has_s0, has_prev, emit_state):
    it = iter(refs)
    q_ref, k_ref, v_ref, rows_ref, bias_ref, alog_ref = (next(it) for _ in range(6))
    cwq_ref, cwk_ref, cwv_ref, cbq_ref, cbk_ref, cbv_ref = (next(it) for _ in range(6))
    s0_ref = next(it) if has_s0 else None
    if has_prev:
        next(it)
    o_ref = next(it)
    sout_ref = next(it) if emit_state else None
    a_ref, b_ref, c_ref, d_ref, el_ref, part_ref, mk_ref, qq_ref, kt_ref, rhs_ref, st_ref = (
        next(it) for _ in range(11))

    ii, jj = _chunk_consts()
    eye = (ii == jj).astype(F32)
    masks = _merge_masks(ii, jj)
    for lvl, mask in enumerate(masks[1:]):
        mk_ref[lvl] = mask.astype(BF16)
    keep = ((ii >= jj), (ii <= jj))
    cum = ((ii <= jj).astype(F32), (ii >= jj).astype(F32))
    edge = (CHUNK - 1, 0)
    chains = [(d, j) for d in range(2) for j in range(2)]

    if has_s0:
        st_ref[...] = s0_ref[...]
    else:
        st_ref[...] = jnp.zeros(st_ref.shape, F32)

    prep = min(GDN_PREP_CHUNKS, n_chunks)
    assert n_chunks % prep == 0

    def setup(p):
        work = []
        cw = [r[...] for r in (cwq_ref, cwk_ref, cwv_ref)]
        cb = [r[...] for r in (cbq_ref, cbk_ref, cbv_ref)]
        alog, bias = alog_ref[...], bias_ref[...]
        loaded = []
        for cc in range(prep):
            c = p * prep + cc
            rows = pl.ds(pl.multiple_of(c * CHUNK, CHUNK), CHUNK)
            loaded.append(([_conv_load(r, c, n_chunks) for r in (q_ref, k_ref, v_ref)], rows_ref[:, rows]))
        for cc in range(prep):
            c = p * prep + cc
            (lq, lk, lv), raw = loaded[cc]
            qc = _conv_apply(lq, c, n_chunks, cw[0], cb[0])
            kc = _conv_apply(lk, c, n_chunks, cw[1], cb[1])
            vc = _conv_apply(lv, c, n_chunks, cw[2], cb[2])
            qn = qc * (lax.rsqrt(jnp.sum(qc * qc, axis=-1, keepdims=True) + RMS_EPS) * (GDN_DK ** -0.5))
            kn = kc * lax.rsqrt(jnp.sum(kc * kc, axis=-1, keepdims=True) + RMS_EPS)
            knt = kn.T
            kntb = knt.astype(BF16)
            kk = _dot(kn.astype(BF16), kntb)
            qk = _dot(qn.astype(BF16), kntb)
            beta = jax.nn.sigmoid(raw)
            gl = -jnp.exp(alog) * _softplus(raw + bias)
            gcum = [_dot(gl, cum[d], precision=HIGHEST) for d in range(2)]
            cols = _rows_to_cols(beta, gcum[0], gcum[1])
            for d in range(2):
                for j in range(2):
                    rg = 4 + 2 * d + j
                    g_r = gcum[d][rg:rg + 1, :]
                    b_f = jnp.broadcast_to(cols[:, 2 * d + j:2 * d + j + 1], (CHUNK, LANES))
                    g_f = jnp.broadcast_to(cols[:, 8 + 8 * d + rg:8 + 8 * d + rg + 1], (CHUNK, LANES))
                    dec = jnp.exp(jnp.where(keep[d], g_f - g_r, -jnp.inf))
                    m = kk * dec * b_f
                    eg = jnp.exp(g_f)
                    last = gcum[d][rg:rg + 1, edge[d]:edge[d] + 1]
                    ch = 2 * d + j
                    el_ref[ch, c] = jnp.broadcast_to(jnp.exp(last), (1, LANES))
                    i = len(work)
                    qq_ref[i] = jnp.concatenate([qn * eg, qk * dec], axis=1).astype(BF16)
                    kt_ref[i] = (knt * jnp.exp(last - g_r)).astype(BF16)
                    rhs_ref[i] = jnp.concatenate([vc[:, GDN_DV * j:GDN_DV * (j + 1)] * b_f, kn * (b_f * eg)], axis=1)
                    work.append(dict(mb=m.astype(BF16), tinv=eye - jnp.where(masks[0], m, 0.0)))
        return work

    def finish(p, work, level_masks):
        for mk in level_masks:
            for wk in work:
                tb = wk["tinv"].astype(BF16)
                y = _dot(tb, wk["mb"] * mk)
                wk["tinv"] = wk["tinv"] - _dot(y.astype(BF16), tb)
        solb = []
        for i, wk in enumerate(work):
            rhs = rhs_ref[i]
            sol = rhs + _dot((wk["tinv"] - eye).astype(BF16), rhs.astype(BF16))
            solb.append(sol.astype(BF16))
        res = []
        for i in range(len(work)):
            qq = qq_ref[i]
            kq = _dot(jnp.concatenate([kt_ref[i], qq[:, GDN_DK:]], axis=0), solb[i])
            ka, qa = kq[:GDN_DK], kq[GDN_DK:]
            res.append((ka[:, GDN_DV:].astype(BF16), ka[:, :GDN_DV],
                        (qq[:, :GDN_DK].astype(F32) - qa[:, GDN_DV:]).astype(BF16), qa[:, :GDN_DV].astype(BF16)))
        for i, (a, b, c_, d_) in enumerate(res):
            rows = pl.ds(pl.multiple_of((p * prep + i // 4) * CHUNK, CHUNK), CHUNK)
            a_ref[i % 4, rows, :] = a
            b_ref[i % 4, rows, :] = b
            c_ref[i % 4, rows, :] = c_
            d_ref[i % 4, rows, :] = d_

    def prepare(p, carry):
        level_masks = [mk_ref[lvl] for lvl in range(len(masks) - 1)]
        finish(p, setup(p), level_masks)
        return carry

    def trip(t, second_half):
        cidx = (t, n_chunks - 1 - t)
        rows = [pl.ds(pl.multiple_of(c * CHUNK, CHUNK), CHUNK) for c in cidx]
        st = [st_ref[d, j] for d, j in chains]
        sb = [s.astype(BF16) for s in st]
        new = [st[2 * d + j] * el_ref[2 * d + j, cidx[d]] + b_ref[2 * d + j, rows[d], :]
               - _dot(a_ref[2 * d + j, rows[d], :], sb[2 * d + j]) for d, j in chains]
        outs = [_dot(c_ref[2 * d + j, rows[d], :], sb[2 * d + j]) + d_ref[2 * d + j, rows[d], :].astype(F32)
                for d, j in chains]
        for d, j in chains:
            st_ref[d, j] = new[2 * d + j]
        for d in range(2):
            o = jnp.concatenate([outs[2 * d], outs[2 * d + 1]], axis=1)
            if second_half:
                o_ref[rows[d], :] = (part_ref[rows[d], :] + o).astype(BF16)
            else:
                part_ref[rows[d], :] = o

    def first_body(t, carry):
        trip(t, False)
        return carry

    def second_body(t, carry):
        trip(t, True)
        return carry

    lax.fori_loop(0, n_chunks // prep, prepare, 0)
    lax.fori_loop(0, n_chunks // 2, first_body, 0)
    lax.fori_loop(n_chunks // 2, n_chunks, second_body, 0)
    if emit_state:
        sout_ref[...] = st_ref[...]


def _gdn_scan(p_main, rows, prm, s0, n_seq, seq_len, row_block0, emit_state, o_prev=None):
    vw = 2 * GDN_DV
    k0 = GDN_KEY_DIM // GDN_DK
    v0 = (2 * GDN_KEY_DIM) // vw
    has_s0 = s0 is not None
    rb = lambda b: row_block0 + b
    st_block = (None, None, 2, 2, GDN_DK, GDN_DV)
    st_map = lambda b, h: (b, 0, 0, h, 0, 0)
    in_specs = [
        pl.BlockSpec((seq_len, GDN_DK), lambda b, h: (rb(b), h)),
        pl.BlockSpec((seq_len, GDN_DK), lambda b, h: (rb(b), k0 + h)),
        pl.BlockSpec((seq_len, vw), lambda b, h: (rb(b), v0 + h)),
        pl.BlockSpec((None, 8, seq_len), lambda b, h: (h, 0, rb(b))),
        pl.BlockSpec((None, 8, LANES), lambda b, h: (h, 0, 0)),
        pl.BlockSpec((None, 8, LANES), lambda b, h: (h, 0, 0)),
        pl.BlockSpec((3, GDN_DK), lambda b, h: (0, h)),
        pl.BlockSpec((3, GDN_DK), lambda b, h: (0, k0 + h)),
        pl.BlockSpec((3, vw), lambda b, h: (0, v0 + h)),
        pl.BlockSpec((1, GDN_DK), lambda b, h: (0, h)),
        pl.BlockSpec((1, GDN_DK), lambda b, h: (0, k0 + h)),
        pl.BlockSpec((1, vw), lambda b, h: (0, v0 + h)),
    ]
    args = [p_main, p_main, p_main, rows, prm["bias"], prm["alog"],
            prm["conv_w"], prm["conv_w"], prm["conv_w"], prm["conv_b"], prm["conv_b"], prm["conv_b"]]
    if has_s0:
        in_specs.append(pl.BlockSpec(st_block, st_map))
        args.append(s0)
    aliases = {}
    if o_prev is not None:
        in_specs.append(pl.BlockSpec(memory_space=pl.ANY))
        args.append(o_prev)
        aliases = {len(args) - 1: 0}
    out_specs = [pl.BlockSpec((seq_len, vw), lambda b, h: (rb(b), h))]
    out_shape = [jax.ShapeDtypeStruct((p_main.shape[0], GDN_VAL_DIM), BF16)]
    if emit_state:
        out_specs.append(pl.BlockSpec(st_block, st_map))
        out_shape.append(jax.ShapeDtypeStruct((n_seq, 1, 2, GDN_V_HEADS, GDN_DK, GDN_DV), F32))
    assert seq_len % (2 * CHUNK) == 0
    outs = pl.pallas_call(
        functools.partial(_gdn_kernel, n_chunks=seq_len // CHUNK, has_s0=has_s0, has_prev=o_prev is not None,
                          emit_state=emit_state),
        grid=(n_seq, GDN_K_HEADS),
        in_specs=in_specs,
        out_specs=out_specs,
        out_shape=out_shape,
        input_output_aliases=aliases,
        scratch_shapes=[
            pltpu.VMEM((4, seq_len, GDN_DK), BF16),
            pltpu.VMEM((4, seq_len, GDN_DV), F32),
            pltpu.VMEM((4, seq_len, GDN_DK), BF16),
            pltpu.VMEM((4, seq_len, GDN_DV), BF16),
            pltpu.VMEM((4, seq_len // CHUNK, 1, LANES), F32),
            pltpu.VMEM((seq_len, vw), F32),
            pltpu.VMEM((CHUNK.bit_length() - 2, CHUNK, CHUNK), BF16),
            pltpu.VMEM((4 * GDN_PREP_CHUNKS, CHUNK, GDN_DK + CHUNK), BF16),
            pltpu.VMEM((4 * GDN_PREP_CHUNKS, GDN_DK, CHUNK), BF16),
            pltpu.VMEM((4 * GDN_PREP_CHUNKS, CHUNK, GDN_DV + GDN_DK), F32),
            pltpu.VMEM((2, 2, GDN_DK, GDN_DV), F32),
        ],
        compiler_params=_cparams(("arbitrary", "arbitrary"), MOE_VMEM_LIMIT),
        name="gdn_scan",
    )(*args)
    return outs if emit_state else (outs[0], None)


def _route(logits_t, rbias):
    aff = jax.nn.sigmoid(logits_t)
    sel = aff + rbias
    a = [aff[e:e + 1, :] for e in range(N_EXPERTS)]
    s = [sel[e:e + 1, :] for e in range(N_EXPERTS)]
    gs = []
    for g in range(N_EXPERT_GROUPS):
        v = s[EXPERTS_PER_GROUP * g:EXPERTS_PER_GROUP * (g + 1)]
        best = None
        for x in range(EXPERTS_PER_GROUP):
            for y in range(x + 1, EXPERTS_PER_GROUP):
                ps = v[x] + v[y]
                best = ps if best is None else jnp.maximum(best, ps)
        gs.append(best)
    top, gi = gs[0], jnp.zeros_like(gs[0], dtype=jnp.int32)
    for g in range(1, N_EXPERT_GROUPS):
        up = gs[g] > top
        top = jnp.where(up, gs[g], top)
        gi = jnp.where(up, g, gi)
    ms = [jnp.where(gi == e // EXPERTS_PER_GROUP, s[e], -jnp.inf) for e in range(N_EXPERTS)]
    b1, i1 = ms[0], jnp.zeros_like(gi)
    for e in range(1, N_EXPERTS):
        up = ms[e] > b1
        b1 = jnp.where(up, ms[e], b1)
        i1 = jnp.where(up, e, i1)
    b2, i2 = jnp.full_like(b1, -jnp.inf), jnp.zeros_like(gi)
    for e in range(N_EXPERTS):
        cand = jnp.where(i1 == e, -jnp.inf, ms[e])
        up = cand > b2
        b2 = jnp.where(up, cand, b2)
        i2 = jnp.where(up, e, i2)
    w1 = sum(jnp.where(i1 == e, a[e], 0.0) for e in range(N_EXPERTS))
    w2 = sum(jnp.where(i2 == e, a[e], 0.0) for e in range(N_EXPERTS))
    tot = w1 + w2
    rows = [jnp.where(i1 == e, w1, 0.0) / tot + jnp.where(i2 == e, w2, 0.0) / tot for e in range(N_EXPERTS)]
    return jnp.concatenate(rows, axis=0)


def _post_kernel(y_ref, z_ref, nw_ref, wo_ref, x_ref, mod_ref, n2_ref, wrt_ref, rb_ref,
                 x1_ref, hf_ref, gate_ref, *, group_width, gate_before_norm):
    y = y_ref[...].astype(F32)
    gz = _silu(z_ref[...].astype(F32))
    if gate_before_norm:
        y = y * gz
    parts = [_rms_unit(y[:, s:s + group_width]) for s in range(0, y.shape[1], group_width)]
    yn = jnp.concatenate(parts, axis=1) * nw_ref[...]
    if not gate_before_norm:
        yn = yn * gz
    out = _dot(yn.astype(BF16), wo_ref[...])
    x1 = x_ref[...] + mod_ref[2:3, :] * out
    x1_ref[...] = x1
    hf = _rms_unit(x1) * n2_ref[...] * (1.0 + mod_ref[4:5, :]) + mod_ref[3:4, :]
    hf_ref[...] = hf.astype(BF16)
    logits_t = _dot_nt(wrt_ref[...], hf, precision=HIGHEST)
    gate_ref[...] = _route(logits_t, rb_ref[...])


def _post(y, p_main, z_block0, x, mods, layer, seg_of, norm_w_full, w_out, norm2_w, w_router_t, router_bias,
          group_width, gate_before_norm):
    t, d = x.shape
    inner = y.shape[1]
    tm = seg_of.tile
    return pl.pallas_call(
        functools.partial(_post_kernel, group_width=group_width, gate_before_norm=gate_before_norm),
        grid=(t // tm,),
        in_specs=[
            pl.BlockSpec((tm, inner), lambda i: (i, 0)),
            pl.BlockSpec((tm, inner), lambda i: (i, z_block0)),
            pl.BlockSpec((1, inner), lambda i: (0, 0)),
            pl.BlockSpec((inner, d), lambda i: (0, 0)),
            pl.BlockSpec((tm, d), lambda i: (i, 0)),
            pl.BlockSpec((None, None, 6, d), lambda i: (layer, seg_of(i), 0, 0)),
            pl.BlockSpec((1, d), lambda i: (0, 0)),
            pl.BlockSpec((N_EXPERTS, d), lambda i: (0, 0)),
            pl.BlockSpec((N_EXPERTS, 1), lambda i: (0, 0)),
        ],
        out_specs=[
            pl.BlockSpec((tm, d), lambda i: (i, 0)),
            pl.BlockSpec((tm, d), lambda i: (i, 0)),
            pl.BlockSpec((N_EXPERTS, tm), lambda i: (0, i)),
        ],
        out_shape=[
            jax.ShapeDtypeStruct((t, d), F32),
            jax.ShapeDtypeStruct((t, d), BF16),
            jax.ShapeDtypeStruct((N_EXPERTS, t), F32),
        ],
        compiler_params=_cparams(("arbitrary",)),
        name="post",
    )(y, p_main, norm_w_full.reshape(1, inner), w_out, x, mods, norm2_w.reshape(1, d), w_router_t,
      router_bias.reshape(N_EXPERTS, 1))


def _moe_kernel(hf_ref, gate_ref, tri_ref, wg_ref, wu_ref, wd_ref, x1_ref, mod_ref, fn_ref, o_ref,
                acc_ref, rank_ref, cnt_ref, *, final_norm, n_ctx_tiles):
    e = pl.program_id(1)
    tm = hf_ref.shape[0]

    @pl.when(e == 0)
    def _():
        acc_ref[...] = jnp.zeros(acc_ref.shape, F32)
        sel = jnp.where(gate_ref[...] > 0.0, 1.0, 0.0)
        rank_ref[...] = _dot(sel.astype(BF16), tri_ref[...])
        for k in range(N_EXPERTS):
            cnt_ref[k] = jnp.sum(sel[k:k + 1, :]).astype(jnp.int32)

    sub = lax.broadcasted_iota(jnp.int32, (MOE_ROWS, tm), 0).astype(F32)
    for k in range(MOE_EXPERTS_PER_STEP):
        ex = e * MOE_EXPERTS_PER_STEP + k
        g_row = gate_ref[pl.ds(ex, 1), :]
        slot = jnp.where(g_row > 0.0, rank_ref[pl.ds(ex, 1), :], -1.0)

        def block(b, carry, k=k, g_row=g_row, slot=slot):
            pf = jnp.where(sub == slot - (b * MOE_ROWS).astype(F32), 1.0, 0.0)
            pb = pf.astype(BF16)
            xg = _dot(pb, hf_ref[...]).astype(BF16)
            hid = _silu(_dot(xg, wg_ref[k])) * _dot(xg, wu_ref[k])
            y = _dot(hid.astype(BF16), wd_ref[k])
            gsel = jnp.sum(pf * g_row, axis=1, keepdims=True)
            acc_ref[...] += _dot_tn(pb, (y * gsel).astype(BF16))
            return carry

        lax.fori_loop(0, (cnt_ref[ex] + MOE_ROWS - 1) // MOE_ROWS, block, 0)

    @pl.when(e == pl.num_programs(1) - 1)
    def _():
        x2 = x1_ref[...] + mod_ref[5:6, :] * acc_ref[...]
        if not final_norm:
            o_ref[0][...] = x2
        else:
            y = _rms_unit(x2) * fn_ref[...]
            i = pl.program_id(0)

            @pl.when(i < n_ctx_tiles)
            def _():
                o_ref[0][...] = y

            @pl.when(i >= n_ctx_tiles)
            def _():
                o_ref[1][...] = y


def _moe_kernel_entry(*refs, final_norm, n_ctx_tiles):
    n_out = 2 if final_norm else 1
    _moe_kernel(*refs[:9], refs[9:9 + n_out], *refs[9 + n_out:], final_norm=final_norm, n_ctx_tiles=n_ctx_tiles)


def _moe(hf, gates, wg, wu, wd, x1, mods, layer, seg_of, final_norm_w, final_norm):
    t, d = x1.shape
    n_e, _, de = wg.shape
    tm = seg_of.tile
    nct = seg_of.n_ctx_tiles
    earlier = jnp.triu(jnp.ones((tm, tm), BF16), k=1)
    if final_norm:
        out_specs = [pl.BlockSpec((tm, d), lambda i, e: (jnp.minimum(i, nct - 1), 0)),
                     pl.BlockSpec((tm, d), lambda i, e: (jnp.maximum(i - nct, 0), 0))]
        out_shape = [jax.ShapeDtypeStruct((nct * tm, d), F32), jax.ShapeDtypeStruct((t - nct * tm, d), F32)]
    else:
        out_specs = [pl.BlockSpec((tm, d), lambda i, e: (i, 0))]
        out_shape = [jax.ShapeDtypeStruct((t, d), F32)]
    return pl.pallas_call(
        functools.partial(_moe_kernel_entry, final_norm=final_norm, n_ctx_tiles=nct),
        grid=(t // tm, n_e // MOE_EXPERTS_PER_STEP),
        in_specs=[
            pl.BlockSpec((tm, d), lambda i, e: (i, 0)),
            pl.BlockSpec((N_EXPERTS, tm), lambda i, e: (0, i)),
            pl.BlockSpec((tm, tm), lambda i, e: (0, 0)),
            pl.BlockSpec((MOE_EXPERTS_PER_STEP, d, de), lambda i, e: (e, 0, 0)),
            pl.BlockSpec((MOE_EXPERTS_PER_STEP, d, de), lambda i, e: (e, 0, 0)),
            pl.BlockSpec((MOE_EXPERTS_PER_STEP, de, d), lambda i, e: (e, 0, 0)),
            pl.BlockSpec((tm, d), lambda i, e: (i, 0)),
            pl.BlockSpec((None, None, 6, d), lambda i, e: (layer, seg_of(i), 0, 0)),
            pl.BlockSpec((1, d), lambda i, e: (0, 0)),
        ],
        out_specs=out_specs,
        out_shape=out_shape,
        scratch_shapes=[
            pltpu.VMEM((tm, d), F32),
            pltpu.VMEM((N_EXPERTS, tm), F32),
            pltpu.SMEM((N_EXPERTS,), jnp.int32),
        ],
        compiler_params=_cparams(("arbitrary", "arbitrary"), MOE_VMEM_LIMIT),
        name="moe",
    )(hf, gates, earlier, wg, wu, wd, x1, mods, final_norm_w.reshape(1, d))


def _lanes(v):
    return jnp.broadcast_to(v[..., None].astype(F32), v.shape + (LANES,))


def _ssd_small_order():
    idx = []
    for g in range(SSD_GROUPS):
        for d in range(2):
            for h in range(SSD_HEADS_PER_GROUP):
                idx.append(d * SSD_HEADS + SSD_HEADS_PER_GROUP * g + h)
    return jnp.array(idx, jnp.int32)


def _gdn_small_order():
    idx = []
    for kh in range(GDN_K_HEADS):
        for which in range(2):
            for d in range(2):
                for j in range(2):
                    idx.append(d * 2 * GDN_V_HEADS + which * GDN_V_HEADS + 2 * kh + j)
    return jnp.array(idx, jnp.int32)


def _ssd_params(j, ssd_w_in, ssd_conv_w, ssd_conv_b, ssd_a_log, ssd_dt_bias, ssd_d):
    order = _ssd_small_order()
    w = ssd_w_in[j]
    per_row = lambda v: v.reshape(-1)[order].reshape(SSD_GROUPS, 2 * SSD_HEADS_PER_GROUP)
    return dict(
        w_main=w[:, :SSD_MAIN].astype(BF16),
        w_small_t=w[:, SSD_MAIN:][:, order].T.astype(BF16),
        conv_w=ssd_conv_w[j], conv_b=ssd_conv_b[j].reshape(1, -1),
        bias=_lanes(per_row(ssd_dt_bias[j])), alog=_lanes(per_row(ssd_a_log[j])),
        dskip=jnp.repeat(ssd_d[j].astype(F32), SSD_HEADDIM).reshape(1, SSD_INNER),
    )


def _gdn_params(j, gdn_w_in, gdn_conv_w, gdn_conv_b, gdn_a_log, gdn_dt_bias):
    order = _gdn_small_order()
    w = gdn_w_in[j]
    def per_row(v):
        full = jnp.stack([jnp.zeros_like(v), v], axis=1).astype(F32)
        return full.reshape(-1)[order].reshape(GDN_K_HEADS, 8)
    return dict(
        w_main=w[:, :GDN_MAIN].astype(BF16),
        w_small_t=w[:, GDN_MAIN:][:, order].T.astype(BF16),
        conv_w=gdn_conv_w[j], conv_b=gdn_conv_b[j].reshape(1, -1),
        bias=_lanes(per_row(gdn_dt_bias[j])), alog=_lanes(per_row(gdn_a_log[j])),
    )


def _ssd_state_in(s):
    n = s.shape[0]
    s = s.reshape(n, 2, SSD_GROUPS, SSD_PAIRS_PER_GROUP, 2, SSD_HEADDIM, SSD_STATE)
    return s.transpose(0, 1, 2, 3, 6, 4, 5).reshape(n, 2, SSD_GROUPS, SSD_PAIRS_PER_GROUP, SSD_STATE, LANES)


def _ssd_state_out(s):
    n = s.shape[0]
    s = s.reshape(n, 2, SSD_GROUPS, SSD_PAIRS_PER_GROUP, SSD_STATE, 2, SSD_HEADDIM)
    return s.transpose(0, 1, 2, 3, 5, 6, 4).reshape(n, 2, SSD_HEADS, SSD_HEADDIM, SSD_STATE)


def kernel(x_prompt, x_sample, state_ssd, state_gdn, c, c_ctx, w_ada, b_ada, norm_w, final_norm_w, ssd_w_in, ssd_conv_w, ssd_conv_b, ssd_a_log, ssd_dt_bias, ssd_d, ssd_norm_w, ssd_w_out, gdn_w_in, gdn_conv_w, gdn_conv_b, gdn_a_log, gdn_dt_bias, gdn_norm_w, gdn_w_out, w_router, router_bias, moe_w_gate, moe_w_up, moe_w_down):
    nb, lc, d = x_prompt.shape
    ns, ls, _ = x_sample.shape
    depth = w_ada.shape[0]
    tc, ts = nb * lc, ns * ls
    assert tc % ls == 0 and ns + 1 <= N_MOD_ROWS

    x = _embed(x_prompt, x_sample)
    cond = jnp.concatenate([c_ctx[None, :], c, jnp.zeros((N_MOD_ROWS - 1 - ns, d), F32)], axis=0)
    mods = _ada(cond, w_ada, b_ada).reshape(depth, N_MOD_ROWS, 6, d)

    seg_big = _SegOf(_tile(1024, tc, ls), tc, ls)
    seg_mid = _SegOf(_tile(512, tc, ls), tc, ls)
    w_router_t = w_router.T.astype(F32)
    ssd_states, gdn_states = [], []
    for i in range(depth):
        j = i // 2
        if i % 2 == 0:
            prm = _ssd_params(j, ssd_w_in, ssd_conv_w, ssd_conv_b, ssd_a_log, ssd_dt_bias, ssd_d)
            p_main, p_small_t = _proj(x, mods, i, seg_mid, norm_w[i, 0], prm["w_main"], prm["w_small_t"])
            rows = p_small_t.reshape(SSD_GROUPS, 2 * SSD_HEADS_PER_GROUP, tc + ts)
            y, s_ctx = _ssd_scan(p_main, rows, prm, None, nb, lc, 0, True)
            y, _ = _ssd_scan(p_main, rows, prm, _ssd_state_in(state_ssd[:, j].astype(F32)), ns, ls,
                             tc // ls, False, y_prev=y)
            ssd_states.append(_ssd_state_out(s_ctx))
            x, hf, gates = _post(y, p_main, 0, x, mods, i, seg_mid, ssd_norm_w[j], ssd_w_out[j].astype(BF16),
                                 norm_w[i, 1], w_router_t, router_bias, SSD_GROUP_WIDTH, True)
        else:
            prm = _gdn_params(j, gdn_w_in, gdn_conv_w, gdn_conv_b, gdn_a_log, gdn_dt_bias)
            p_main, p_small_t = _proj(x, mods, i, seg_mid, norm_w[i, 0], prm["w_main"], prm["w_small_t"])
            rows = p_small_t.reshape(GDN_K_HEADS, 8, tc + ts)
            y, s_ctx = _gdn_scan(p_main, rows, prm, None, nb, lc, 0, True)
            y, _ = _gdn_scan(p_main, rows, prm, state_gdn[:, j:j + 1].astype(F32), ns, ls, tc // ls, False,
                             o_prev=y)
            gdn_states.append(s_ctx[:, 0])
            x, hf, gates = _post(y, p_main, GDN_CONV_DIM // GDN_VAL_DIM, x, mods, i, seg_mid,
                                 jnp.tile(gdn_norm_w[j], GDN_V_HEADS), gdn_w_out[j].astype(BF16),
                                 norm_w[i, 1], w_router_t, router_bias, GDN_DV, False)
        outs = _moe(hf, gates, moe_w_gate[i].astype(BF16), moe_w_up[i].astype(BF16), moe_w_down[i].astype(BF16),
                    x, mods, i, seg_big, final_norm_w, i == depth - 1)
        x = outs[0]
    y_prompt = outs[0].reshape(nb, lc, d)
    y_sample = outs[1].reshape(ns, ls, d)
    return (y_prompt, y_sample, jnp.stack(ssd_states, axis=1), jnp.stack(gdn_states, axis=1))
```
